```python
import jax, jax.numpy as jnp
from jax import lax
import numpy as np

D_MODEL = 1024
BATCH = 16
SEQ = 4096
DEPTH = 4

GRID_W = 64
CTX_LEN = 256
N_MOD = 6
EPS = 1e-6
ROPE_BASE = 10000.0
QBLK = 128
NEG_INF = -1e30
A_HEADS = 12
A_NOPE = 64
A_ROPE = 32
A_QK = A_NOPE + A_ROPE
A_V = 64
A_Q_LORA = 256
A_KV_LORA = 128
A_IN = A_Q_LORA + A_KV_LORA + A_ROPE
B_GROUPS = 4
B_GDIM = 64
B_WIDTH = B_GROUPS * B_GDIM
EVEN_IN = A_IN + B_WIDTH
EVEN_MIX = A_HEADS * A_V + B_WIDTH
C_HEADS = 12
C_KV_HEADS = 4
C_GROUP = C_HEADS // C_KV_HEADS
C_HDIM = 64
C_WINDOW = 128
C_QW = C_HEADS * C_HDIM
C_KW = C_KV_HEADS * C_HDIM
D_GROUPS = 4
D_GDIM = 64
D_WIDTH = D_GROUPS * D_GDIM
D_WINDOWS = (2, 4, 8, 16)
ODD_IN = C_QW + 2 * C_KW + D_WIDTH
ODD_MIX = C_QW + D_WIDTH
D_FF = 2816
CONV_W = 3
N_EVEN = (DEPTH + 1) // 2
N_ODD = DEPTH // 2

kernel_name = 'hybrid_mla_fnet_swa_pool_dit'


def rmsnorm(x, g):
    xf = x.astype(jnp.float32)
    y = xf * lax.rsqrt(jnp.mean(xf * xf, axis=-1, keepdims=True) + EPS)
    return (y * g.astype(jnp.float32)).astype(x.dtype)


def modulate(u, shift, scale):
    return u * (1 + scale) + shift


def axial_rope_tables(rows, rot_dim, dtype):
    n_freq = rot_dim // 4
    inv = ROPE_BASE ** (-jnp.arange(n_freq, dtype=jnp.float32) / n_freq)
    row = jnp.broadcast_to(jnp.arange(rows, dtype=jnp.float32)[:, None], (rows, GRID_W)).reshape(-1)
    col = jnp.broadcast_to(jnp.arange(GRID_W, dtype=jnp.float32)[None, :], (rows, GRID_W)).reshape(-1)
    ang = jnp.concatenate([row[:, None] * inv, col[:, None] * inv], axis=-1)
    return jnp.cos(ang).astype(dtype), jnp.sin(ang).astype(dtype)


def apply_rope(x, cos, sin):
    h = x.shape[-1] // 2
    x1, x2 = x[..., :h], x[..., h:]
    cs, sn = cos[None, :, None, :], sin[None, :, None, :]
    return jnp.concatenate([x1 * cs - x2 * sn, x1 * sn + x2 * cs], axis=-1)


def attend_grouped(q, k, v, scale, sink=None):
    s = jnp.einsum('bqkgd,bskd->bkgqs', q, k).astype(jnp.float32) * scale
    if sink is not None:
        sk = jnp.broadcast_to(sink.astype(jnp.float32)[None, :, :, None, None], s.shape[:-1] + (1,))
        s = jnp.concatenate([s, sk], axis=-1)
    p = jax.nn.softmax(s, axis=-1)[..., :k.shape[1]].astype(v.dtype)
    return jnp.einsum('bkgqs,bskd->bqkgd', p, v)


def latent_dense_attention(q, k, v, k_ctx, v_ctx, scale):
    B, L, H, dq = q.shape
    nb = L // QBLK
    n_ctx = k_ctx.shape[1]
    qb = jnp.moveaxis(q.reshape(B, nb, QBLK, H, dq), 1, 0)

    def block(qi):
        s = jnp.concatenate([jnp.einsum('bqhd,bchd->bhqc', qi, k_ctx),
                             jnp.einsum('bqhd,bkhd->bhqk', qi, k)], axis=-1).astype(jnp.float32) * scale
        p = jax.nn.softmax(s, axis=-1).astype(v.dtype)
        return (jnp.einsum('bhqc,bchd->bqhd', p[..., :n_ctx], v_ctx)
                + jnp.einsum('bhqk,bkhd->bqhd', p[..., n_ctx:], v))

    o = lax.map(block, qb)
    return jnp.moveaxis(o, 0, 1).reshape(B, L, -1)


def window_attention(q, k, v, k_ctx, v_ctx, sink, scale):
    B, L = q.shape[:2]
    nb = L // QBLK
    n_ctx = k_ctx.shape[1]
    span = 3 * QBLK
    pad = ((0, 0), (QBLK, QBLK), (0, 0), (0, 0))
    k_pad, v_pad = jnp.pad(k, pad), jnp.pad(v, pad)
    qb = jnp.moveaxis(q.reshape((B, nb, QBLK) + q.shape[2:]), 1, 0)
    rel = jnp.arange(span)[None, :] - QBLK - jnp.arange(QBLK)[:, None]
    band = jnp.abs(rel) <= C_WINDOW
    sink_col = jnp.broadcast_to(sink.astype(jnp.float32)[None, :, :, None, None], (B, C_KV_HEADS, C_GROUP, QBLK, 1))

    def block(args):
        i, qi = args
        start = i * QBLK
        ki = lax.dynamic_slice_in_dim(k_pad, start, span, axis=1)
        vi = lax.dynamic_slice_in_dim(v_pad, start, span, axis=1)
        key_pos = start - QBLK + jnp.arange(span)
        valid = band & ((key_pos >= 0) & (key_pos < L))[None, :]
        s_ctx = jnp.einsum('bqkgd,bckd->bkgqc', qi, k_ctx).astype(jnp.float32) * scale
        s_win = jnp.einsum('bqkgd,bskd->bkgqs', qi, ki).astype(jnp.float32) * scale
        s_win = jnp.where(valid, s_win, NEG_INF)
        p = jax.nn.softmax(jnp.concatenate([s_ctx, s_win, sink_col], axis=-1), axis=-1).astype(v.dtype)
        return (jnp.einsum('bkgqc,bckd->bqkgd', p[..., :n_ctx], v_ctx)
                + jnp.einsum('bkgqs,bskd->bqkgd', p[..., n_ctx:n_ctx + span], vi))

    o = lax.map(block, (jnp.arange(nb), qb))
    return jnp.moveaxis(o, 0, 1).reshape(B, L, C_QW)


def fourier_mix(u, w_f):
    B, L, _ = u.shape
    z = u.reshape(B, L, B_GROUPS, B_GDIM).astype(jnp.float32)
    z = jnp.fft.fft2(z, axes=(1, 3), norm='ortho').real.astype(u.dtype)
    return jnp.einsum('blgc,gcd->blgd', z, w_f).reshape(B, L, B_WIDTH)


def multiscale_pool(u, w_p, p_scale):
    B, L, _ = u.shape
    z = u.reshape(B, L, D_GROUPS, D_GDIM).astype(jnp.float32)
    cs = jnp.concatenate([jnp.zeros((B, 1, D_GROUPS, D_GDIM), jnp.float32), jnp.cumsum(z, axis=1)], axis=1)
    t = jnp.arange(L)[:, None]
    half = jnp.asarray(D_WINDOWS, jnp.int32)[None, :] // 2
    lo = jnp.clip(t - half, 0, L)
    hi = jnp.clip(t + half, 0, L)
    grp = jnp.arange(D_GROUPS)[None, :]
    win_sum = cs[:, hi, grp, :] - cs[:, lo, grp, :]
    mean = win_sum / (hi - lo).astype(jnp.float32)[None, :, :, None]
    pooled = (mean - z).astype(u.dtype)
    y = jnp.einsum('blgc,gcd->blgd', pooled, w_p).reshape(B, L, D_WIDTH)
    return y * p_scale


def mla_keys_values(h, ckv_g, w_ukv, k_g):
    B, L, _ = h.shape
    ckv = h[..., A_Q_LORA:A_Q_LORA + A_KV_LORA]
    k_rope = h[..., A_Q_LORA + A_KV_LORA:A_IN]
    kv = (rmsnorm(ckv, ckv_g) @ w_ukv).reshape(B, L, A_HEADS, A_NOPE + A_V)
    k = jnp.concatenate([kv[..., :A_NOPE], jnp.broadcast_to(k_rope[:, :, None, :], (B, L, A_HEADS, A_ROPE))], axis=-1)
    return rmsnorm(k, k_g), kv[..., A_NOPE:]


def mla_queries(h, cq_g, w_uq, q_g):
    B, L, _ = h.shape
    q = (rmsnorm(h[..., :A_Q_LORA], cq_g) @ w_uq).reshape(B, L, A_HEADS, A_QK)
    return rmsnorm(q, q_g)


def rope_tail(x, cos, sin):
    return jnp.concatenate([x[..., :A_NOPE], apply_rope(x[..., A_NOPE:], cos, sin)], axis=-1)


def even_mixer(u_ctx, u_lat, w_in, cq_g, ckv_g, w_uq, w_ukv, q_g, k_g, w_f, w_out, cos, sin, ctx_out):
    scale = A_QK ** -0.5
    B, L, _ = u_lat.shape
    h_ctx = u_ctx @ w_in
    h_lat = u_lat @ w_in
    k_ctx, v_ctx = mla_keys_values(h_ctx, ckv_g, w_ukv, k_g)
    k_lat, v_lat = mla_keys_values(h_lat, ckv_g, w_ukv, k_g)
    q_lat = rope_tail(mla_queries(h_lat, cq_g, w_uq, q_g), cos, sin)
    k_lat = rope_tail(k_lat, cos, sin)
    o_lat = latent_dense_attention(q_lat, k_lat, v_lat, k_ctx, v_ctx, scale)
    y_lat = jnp.concatenate([o_lat, fourier_mix(h_lat[..., A_IN:], w_f)], axis=-1) @ w_out
    y_ctx = None
    if ctx_out:
        Bc, C, _ = u_ctx.shape
        q_ctx = mla_queries(h_ctx, cq_g, w_uq, q_g)[:, :, :, None, :]
        o_ctx = attend_grouped(q_ctx, k_ctx, v_ctx, scale).reshape(Bc, C, -1)
        y_ctx = jnp.concatenate([o_ctx, fourier_mix(h_ctx[..., A_IN:], w_f)], axis=-1) @ w_out
    return y_ctx, y_lat


def gqa_keys_values(h, k_g):
    B, L, _ = h.shape
    k = rmsnorm(h[..., C_QW:C_QW + C_KW].reshape(B, L, C_KV_HEADS, C_HDIM), k_g)
    v = h[..., C_QW + C_KW:C_QW + 2 * C_KW].reshape(B, L, C_KV_HEADS, C_HDIM)
    return k, v


def gqa_queries(h, q_g):
    B, L, _ = h.shape
    return rmsnorm(h[..., :C_QW].reshape(B, L, C_HEADS, C_HDIM), q_g)


def odd_mixer(u_ctx, u_lat, w_in, q_g, k_g, sink, w_p, p_scale, w_out, cos, sin, ctx_out):
    scale = C_HDIM ** -0.5
    B, L, _ = u_lat.shape
    h_ctx = u_ctx @ w_in
    h_lat = u_lat @ w_in
    k_ctx, v_ctx = gqa_keys_values(h_ctx, k_g)
    k_lat, v_lat = gqa_keys_values(h_lat, k_g)
    q_lat = apply_rope(gqa_queries(h_lat, q_g), cos, sin).reshape(B, L, C_KV_HEADS, C_GROUP, C_HDIM)
    k_lat = apply_rope(k_lat, cos, sin)
    o_lat = window_attention(q_lat, k_lat, v_lat, k_ctx, v_ctx, sink, scale)
    y_lat = jnp.concatenate([o_lat, multiscale_pool(h_lat[..., C_QW + 2 * C_KW:], w_p, p_scale)], axis=-1) @ w_out
    y_ctx = None
    if ctx_out:
        Bc, C, _ = u_ctx.shape
        q_ctx = gqa_queries(h_ctx, q_g).reshape(Bc, C, C_KV_HEADS, C_GROUP, C_HDIM)
        o_ctx = attend_grouped(q_ctx, k_ctx, v_ctx, scale, sink).reshape(Bc, C, C_QW)
        y_ctx = jnp.concatenate([o_ctx, multiscale_pool(h_ctx[..., C_QW + 2 * C_KW:], w_p, p_scale)], axis=-1) @ w_out
    return y_ctx, y_lat


def conv_ffn(u, w_up, conv_w, conv_b, w_down):
    h = u @ w_up
    hp = jnp.pad(h, ((0, 0), (1, 1), (0, 0)))
    h = hp[:, :-2] * conv_w[0] + hp[:, 1:-1] * conv_w[1] + hp[:, 2:] * conv_w[2] + conv_b
    return (jax.nn.silu(h[..., :D_FF]) * h[..., D_FF:]) @ w_down


def setup_inputs(seed: int = 0) -> dict:
    key = jax.random.key(seed)
    ks = iter(jax.random.split(key, 32))
    D = D_MODEL

    def nrm(shape, scale):
        return jax.random.normal(next(ks), shape, jnp.float32) * scale

    def gain(shape):
        return 1.0 + nrm(shape, 0.02)

    return {
        'x': nrm((BATCH, SEQ, D), 1.0),
        'c': nrm((BATCH, D), 1.0),
        'ctx': nrm((BATCH, CTX_LEN, D), 1.0),
        'c_ctx': nrm((D,), 1.0),
        'mod_w': nrm((DEPTH, D, N_MOD * D), 0.5 * D ** -0.5),
        'mod_b': nrm((DEPTH, N_MOD * D), 0.01),
        'norm1_g': gain((DEPTH, D)),
        'norm2_g': gain((DEPTH, D)),
        'mla_w_in': nrm((N_EVEN, D, EVEN_IN), D ** -0.5),
        'mla_cq_g': gain((N_EVEN, A_Q_LORA)),
        'mla_ckv_g': gain((N_EVEN, A_KV_LORA)),
        'mla_w_uq': nrm((N_EVEN, A_Q_LORA, A_HEADS * A_QK), A_Q_LORA ** -0.5),
        'mla_w_ukv': nrm((N_EVEN, A_KV_LORA, A_HEADS * (A_NOPE + A_V)), A_KV_LORA ** -0.5),
        'mla_q_g': gain((N_EVEN, A_QK)),
        'mla_k_g': gain((N_EVEN, A_QK)),
        'fnet_w': nrm((N_EVEN, B_GROUPS, B_GDIM, B_GDIM), B_GDIM ** -0.5),
        'even_w_out': nrm((N_EVEN, EVEN_MIX, D), EVEN_MIX ** -0.5),
        'win_w_in': nrm((N_ODD, D, ODD_IN), D ** -0.5),
        'win_q_g': gain((N_ODD, C_HDIM)),
        'win_k_g': gain((N_ODD, C_HDIM)),
        'win_sink': nrm((N_ODD, C_KV_HEADS, C_GROUP), 0.5),
        'pool_w': nrm((N_ODD, D_GROUPS, D_GDIM, D_GDIM), D_GDIM ** -0.5),
        'pool_scale': 0.5 + nrm((N_ODD, D_WIDTH), 0.1),
        'odd_w_out': nrm((N_ODD, ODD_MIX, D), ODD_MIX ** -0.5),
        'ffn_up': nrm((DEPTH, D, 2 * D_FF), D ** -0.5),
        'ffn_conv_w': nrm((DEPTH, CONV_W, 2 * D_FF), CONV_W ** -0.5),
        'ffn_conv_b': nrm((DEPTH, 2 * D_FF), 0.01),
        'ffn_down': nrm((DEPTH, D_FF, D), D_FF ** -0.5),
    }


def reference(x, c, ctx, c_ctx, mod_w, mod_b, norm1_g, norm2_g,
              mla_w_in, mla_cq_g, mla_ckv_g, mla_w_uq, mla_w_ukv, mla_q_g, mla_k_g, fnet_w, even_w_out,
              win_w_in, win_q_g, win_k_g, win_sink, pool_w, pool_scale, odd_w_out,
              ffn_up, ffn_conv_w, ffn_conv_b, ffn_down):
    rows = x.shape[1] // GRID_W
    cos_a, sin_a = axial_rope_tables(rows, A_ROPE, x.dtype)
    cos_w, sin_w = axial_rope_tables(rows, C_HDIM, x.dtype)
    h_ctx = ctx
    for i in range(DEPTH):
        ctx_out = i < DEPTH - 1
        j = i // 2
        m_lat = jnp.split((jax.nn.silu(c) @ mod_w[i] + mod_b[i])[:, None, :], N_MOD, axis=-1)
        m_ctx = jnp.split(jax.nn.silu(c_ctx) @ mod_w[i] + mod_b[i], N_MOD, axis=-1)
        u_lat = modulate(rmsnorm(x, norm1_g[i]), m_lat[0], m_lat[1])
        u_ctx = modulate(rmsnorm(h_ctx, norm1_g[i]), m_ctx[0], m_ctx[1])
        if i % 2 == 0:
            y_ctx, y_lat = even_mixer(u_ctx, u_lat, mla_w_in[j], mla_cq_g[j], mla_ckv_g[j], mla_w_uq[j], mla_w_ukv[j],
                                      mla_q_g[j], mla_k_g[j], fnet_w[j], even_w_out[j], cos_a, sin_a, ctx_out)
        else:
            y_ctx, y_lat = odd_mixer(u_ctx, u_lat, win_w_in[j], win_q_g[j], win_k_g[j], win_sink[j],
                                     pool_w[j], pool_scale[j], odd_w_out[j], cos_w, sin_w, ctx_out)
        x = x + m_lat[2] * y_lat
        x = x + m_lat[5] * conv_ffn(modulate(rmsnorm(x, norm2_g[i]), m_lat[3], m_lat[4]),
                                    ffn_up[i], ffn_conv_w[i], ffn_conv_b[i], ffn_down[i])
        if ctx_out:
            h_ctx = h_ctx + m_ctx[2] * y_ctx
            h_ctx = h_ctx + m_ctx[5] * conv_ffn(modulate(rmsnorm(h_ctx, norm2_g[i]), m_ctx[3], m_ctx[4]),
                                                ffn_up[i], ffn_conv_w[i], ffn_conv_b[i], ffn_down[i])
    return x
```

```python
import functools
import math

import jax
import jax.numpy as jnp
from jax import lax
from jax.experimental import pallas as pl
from jax.experimental.pallas import tpu as pltpu

F32 = jnp.float32
BF16 = jnp.bfloat16

GRID_W = 64
N_MOD = 6
EPS = 1e-6
ROPE_BASE = 10000.0
NEG_INF = -1e30
A_HEADS = 12
A_NOPE = 64
A_ROPE = 32
A_QK = A_NOPE + A_ROPE
A_V = 64
A_Q_LORA = 256
A_KV_LORA = 128
A_IN = A_Q_LORA + A_KV_LORA + A_ROPE
B_GROUPS = 4
B_GDIM = 64
B_WIDTH = B_GROUPS * B_GDIM
C_HEADS = 12
C_KV_HEADS = 4
C_GROUP = C_HEADS // C_KV_HEADS
C_HDIM = 64
C_WINDOW = 128
C_QW = C_HEADS * C_HDIM
C_KW = C_KV_HEADS * C_HDIM
D_GROUPS = 4
D_GDIM = 64
D_WIDTH = D_GROUPS * D_GDIM
D_WINDOWS = (2, 4, 8, 16)

LANES = 128
BF16_ROWS = 16
HEAD_PAD = LANES
TOK_TILE = 256
FFN_HALO = BF16_ROWS
FFN_CHUNK = 256
FNET_BGRP = 4
VMEM_LIMIT = 56 * 1024 * 1024

NT_DIMS = (((1,), (1,)), ((), ()))


def _cparams(sem):
    return pltpu.CompilerParams(dimension_semantics=sem, vmem_limit_bytes=VMEM_LIMIT)


def _mod_norm(x, g, shift, scale):
    ms = jnp.mean(x * x, axis=-1, keepdims=True)
    y = x * lax.rsqrt(ms + EPS) * g
    return y * (1.0 + scale) + shift


def _rms_rows(xT, g_col):
    ms = jnp.mean(xT * xT, axis=0, keepdims=True)
    return xT * lax.rsqrt(ms + EPS) * g_col


def _rope_rows(x1, x2, cos, sin):
    return x1 * cos - x2 * sin, x1 * sin + x2 * cos


def _mod_kernel(cc_ref, w_ref, b_ref, o_ref):
    cc = cc_ref[...]
    a = (cc * jax.nn.sigmoid(cc)).astype(BF16)
    w = w_ref[0].astype(BF16)
    o_ref[0] = jnp.dot(a, w, preferred_element_type=F32) + b_ref[0]


def _modulation(cc, mod_w, mod_b):
    depth, d, n = mod_w.shape
    rows = cc.shape[0]
    tn = 1024
    return pl.pallas_call(
        _mod_kernel,
        grid=(depth, n // tn),
        in_specs=[
            pl.BlockSpec((rows, d), lambda i, j: (0, 0)),
            pl.BlockSpec((1, d, tn), lambda i, j: (i, 0, j)),
            pl.BlockSpec((1, 1, tn), lambda i, j: (i, 0, j)),
        ],
        out_specs=pl.BlockSpec((1, rows, tn), lambda i, j: (i, 0, j)),
        out_shape=jax.ShapeDtypeStruct((depth, rows, n), F32),
        compiler_params=_cparams(("arbitrary", "arbitrary")),
        name="modulation",
    )(cc, mod_w, mod_b.reshape(depth, 1, n))


def _even_in_kernel(x_ref, m_ref, g1_ref, w_in_ref, cqg_ref, wuq_ref, qg_ref, ckvg_ref, wukv_ref, kg_ref,
                    cos_ref, sin_ref, q_ref, k_ref, v_ref, z_ref):
    tm = x_ref.shape[1]
    m = m_ref[0]
    u = _mod_norm(x_ref[0], g1_ref[...], m[0:1], m[1:2]).astype(BF16)
    hT = lax.dot_general(w_in_ref[...], u, NT_DIMS, preferred_element_type=F32)
    cos = cos_ref[...]
    sin = sin_ref[...]
    scale = A_QK ** -0.5
    half = A_ROPE // 2
    zpad = jnp.zeros((HEAD_PAD - A_QK, tm), F32)

    z_ref[0] = hT[A_IN:A_IN + B_WIDTH].T.astype(BF16)

    cqn = _rms_rows(hT[0:A_Q_LORA], cqg_ref[...]).astype(BF16)
    qT = jnp.dot(wuq_ref[...], cqn, preferred_element_type=F32)
    qg = qg_ref[...] * scale
    for h in range(A_HEADS):
        qn = _rms_rows(qT[h * A_QK:(h + 1) * A_QK], qg)
        o1, o2 = _rope_rows(qn[A_NOPE:A_NOPE + half], qn[A_NOPE + half:A_QK], cos, sin)
        full = jnp.concatenate([qn[0:A_NOPE], o1, o2, zpad], axis=0)
        q_ref[0, h] = full.T.astype(BF16)

    ckvn = _rms_rows(hT[A_Q_LORA:A_Q_LORA + A_KV_LORA], ckvg_ref[...]).astype(BF16)
    kvT = jnp.dot(wukv_ref[...], ckvn, preferred_element_type=F32)
    kr = hT[A_Q_LORA + A_KV_LORA:A_IN]
    kr_ss = jnp.sum(kr * kr, axis=0, keepdims=True)
    kg = kg_ref[...]
    hw = A_NOPE + A_V
    for h in range(A_HEADS):
        kn = kvT[h * hw:h * hw + A_NOPE]
        r = lax.rsqrt((jnp.sum(kn * kn, axis=0, keepdims=True) + kr_ss) * (1.0 / A_QK) + EPS)
        knn = kn * r * kg[0:A_NOPE]
        krn = kr * r * kg[A_NOPE:A_QK]
        o1, o2 = _rope_rows(krn[0:half], krn[half:A_ROPE], cos, sin)
        k_ref[0, h] = jnp.concatenate([knn, o1, o2, zpad], axis=0).astype(BF16)
        v_ref[0, h * A_V:(h + 1) * A_V, :] = kvT[h * hw + A_NOPE:(h + 1) * hw].astype(BF16)


def _even_in(x, m, g1, w_inT, cq_g, w_uqT, q_g, ckv_g, w_ukvT, k_g, cosT, sinT, n_lat_tiles):
    B, Lt, D = x.shape
    tm = TOK_TILE
    nt = Lt // tm
    full2 = lambda a: pl.BlockSpec(a.shape, lambda b, t: (0, 0))
    return pl.pallas_call(
        _even_in_kernel,
        grid=(B, nt),
        in_specs=[
            pl.BlockSpec((1, tm, D), lambda b, t: (b, t, 0)),
            pl.BlockSpec((1, N_MOD, D), lambda b, t: (jnp.where(t < n_lat_tiles, b, B), 0, 0)),
            full2(g1), full2(w_inT), full2(cq_g), full2(w_uqT), full2(q_g), full2(ckv_g), full2(w_ukvT), full2(k_g),
            pl.BlockSpec((A_ROPE // 2, tm), lambda b, t: (0, t)),
            pl.BlockSpec((A_ROPE // 2, tm), lambda b, t: (0, t)),
        ],
        out_specs=[
            pl.BlockSpec((1, A_HEADS, tm, HEAD_PAD), lambda b, t: (b, 0, t, 0)),
            pl.BlockSpec((1, A_HEADS, HEAD_PAD, tm), lambda b, t: (b, 0, 0, t)),
            pl.BlockSpec((1, A_HEADS * A_V, tm), lambda b, t: (b, 0, t)),
            pl.BlockSpec((1, tm, B_WIDTH), lambda b, t: (b, t, 0)),
        ],
        out_shape=[
            jax.ShapeDtypeStruct((B, A_HEADS, Lt, HEAD_PAD), BF16),
            jax.ShapeDtypeStruct((B, A_HEADS, HEAD_PAD, Lt), BF16),
            jax.ShapeDtypeStruct((B, A_HEADS * A_V, Lt), BF16),
            jax.ShapeDtypeStruct((B, Lt, B_WIDTH), BF16),
        ],
        compiler_params=_cparams(("parallel", "arbitrary")),
        name="even_in",
    )(x, m, g1, w_inT, cq_g, w_uqT, q_g, ckv_g, w_ukvT, k_g, cosT, sinT)


def _mla_attn_kernel(q_ref, k_ref, v_ref, o_ref, *, n_lat_tiles, n_lat):
    t = pl.program_id(2)
    tq = q_ref.shape[2]
    lane = lax.broadcasted_iota(jnp.int32, (tq, 2 * A_V), 1)

    def attend(lo, hi):
        vT = v_ref[0, :, lo:hi]
        outs = []
        for j in range(2):
            s = jnp.dot(q_ref[0, j], k_ref[0, j, :, lo:hi], preferred_element_type=F32)
            mx = jnp.max(s, axis=-1, keepdims=True)
            p = jnp.exp(s - mx)
            l = jnp.sum(p, axis=-1, keepdims=True)
            o = lax.dot_general(p.astype(BF16), vT, NT_DIMS, preferred_element_type=F32)
            outs.append(o / l)
        o_ref[0] = jnp.where(lane < A_V, outs[0], outs[1]).astype(o_ref.dtype)

    @pl.when(t < n_lat_tiles)
    def _():
        attend(0, k_ref.shape[3])

    @pl.when(t >= n_lat_tiles)
    def _():
        attend(n_lat, k_ref.shape[3])


def _mla_attn(q, kT, vT, n_lat):
    B, H, Lt, _ = q.shape
    tq = TOK_TILE
    return pl.pallas_call(
        functools.partial(_mla_attn_kernel, n_lat_tiles=n_lat // tq, n_lat=n_lat),
        grid=(B, H // 2, Lt // tq),
        in_specs=[
            pl.BlockSpec((1, 2, tq, HEAD_PAD), lambda b, h, t: (b, h, t, 0)),
            pl.BlockSpec((1, 2, HEAD_PAD, Lt), lambda b, h, t: (b, h, 0, 0)),
            pl.BlockSpec((1, 2 * A_V, Lt), lambda b, h, t: (b, h, 0)),
        ],
        out_specs=pl.BlockSpec((1, tq, 2 * A_V), lambda b, h, t: (b, t, h)),
        out_shape=jax.ShapeDtypeStruct((B, Lt, H * A_V), BF16),
        compiler_params=_cparams(("parallel", "arbitrary", "arbitrary")),
        name="mla_attn",
    )(q, kT, vT)


def _fnet_kernel(*refs, norm, aliased):
    if aliased:
        refs = refs[1:]
    c_ref, s_ref, z_ref, cs_ref, wf_ref, o_ref, p_acc, q_acc = refs
    k = pl.program_id(2)

    @pl.when(k == 0)
    def _():
        p_acc[...] = jnp.zeros_like(p_acc)
        q_acc[...] = jnp.zeros_like(q_acc)

    c = c_ref[...]
    s = s_ref[...]
    for j in range(z_ref.shape[0]):
        z = z_ref[j]
        p_acc[j] += jnp.dot(c, z, preferred_element_type=F32)
        q_acc[j] += jnp.dot(s, z, preferred_element_type=F32)

    @pl.when(k == pl.num_programs(2) - 1)
    def _():
        for j in range(z_ref.shape[0]):
            pq = jnp.concatenate([p_acc[j], q_acc[j]], axis=1).astype(BF16)
            y = jnp.dot(pq, cs_ref[...], preferred_element_type=F32) * norm
            o_ref[j] = jnp.dot(y.astype(BF16), wf_ref[...], preferred_element_type=F32).astype(o_ref.dtype)


def _fnet(z, c_tab, s_tab, cs_bd, wf_bd, row0, out_rows, prev_out=None):
    B = z.shape[0]
    n = c_tab.shape[0]
    tm = min(n, 1024)
    tk = min(n, 1024)
    g = FNET_BGRP
    norm = 1.0 / math.sqrt(n * B_GDIM)
    aliased = prev_out is not None
    in_specs = [
        pl.BlockSpec((tm, tk), lambda b, i, k: (i, k)),
        pl.BlockSpec((tm, tk), lambda b, i, k: (i, k)),
        pl.BlockSpec((g, tk, B_WIDTH), lambda b, i, k: (b, row0 // tk + k, 0)),
        pl.BlockSpec(cs_bd.shape, lambda b, i, k: (0, 0)),
        pl.BlockSpec(wf_bd.shape, lambda b, i, k: (0, 0)),
    ]
    args = [c_tab, s_tab, z, cs_bd, wf_bd]
    if aliased:
        in_specs = [pl.BlockSpec(memory_space=pl.ANY)] + in_specs
        args = [prev_out] + args
    return pl.pallas_call(
        functools.partial(_fnet_kernel, norm=norm, aliased=aliased),
        grid=(B // g, n // tm, n // tk),
        in_specs=in_specs,
        out_specs=pl.BlockSpec((g, tm, B_WIDTH), lambda b, i, k: (b, row0 // tm + i, 0)),
        out_shape=jax.ShapeDtypeStruct((B, out_rows, B_WIDTH), BF16),
        scratch_shapes=[pltpu.VMEM((g, tm, B_WIDTH), F32), pltpu.VMEM((g, tm, B_WIDTH), F32)],
        input_output_aliases={0: 0} if aliased else {},
        compiler_params=_cparams(("parallel", "arbitrary", "arbitrary")),
        name="fnet",
    )(*args)


def _odd_in_kernel(x_ref, m_ref, g1_ref, w_in_ref, qg_ref, kg_ref, cos_ref, sin_ref, q_ref, k_ref, v_ref, pz_ref):
    tm = x_ref.shape[1]
    m = m_ref[0]
    u = _mod_norm(x_ref[0], g1_ref[...], m[0:1], m[1:2]).astype(BF16)
    hT = lax.dot_general(w_in_ref[...], u, NT_DIMS, preferred_element_type=F32)
    cos = cos_ref[...]
    sin = sin_ref[...]
    half = C_HDIM // 2
    zeros = jnp.zeros((C_HDIM, tm), F32)

    def norm_rope(xT, g_col):
        xn = _rms_rows(xT, g_col)
        o1, o2 = _rope_rows(xn[0:half], xn[half:C_HDIM], cos, sin)
        return jnp.concatenate([o1, o2], axis=0)

    qg = qg_ref[...] * (C_HDIM ** -0.5)
    for h in range(C_HEADS):
        qh = norm_rope(hT[h * C_HDIM:(h + 1) * C_HDIM], qg)
        parts = [qh, zeros] if (h // C_GROUP) % 2 == 0 else [zeros, qh]
        q_ref[0, h] = jnp.concatenate(parts, axis=0).T.astype(BF16)

    kg = kg_ref[...]
    for p in range(C_KV_HEADS // 2):
        ks = [norm_rope(hT[C_QW + (2 * p + e) * C_HDIM:C_QW + (2 * p + e + 1) * C_HDIM], kg) for e in range(2)]
        k_ref[0, p] = jnp.concatenate(ks, axis=0).T.astype(BF16)
        v0 = C_QW + C_KW + 2 * p * C_HDIM
        v_ref[0, p] = hT[v0:v0 + 2 * C_HDIM].T.astype(BF16)

    pz_ref[0] = hT[C_QW + 2 * C_KW:C_QW + 2 * C_KW + D_WIDTH].T


def _odd_in(x, m, g1, w_inT, q_g, k_g, cosT, sinT, n_lat_tiles):
    B, Lt, D = x.shape
    tm = TOK_TILE
    nt = Lt // tm
    full2 = lambda a: pl.BlockSpec(a.shape, lambda b, t: (0, 0))
    np_ = C_KV_HEADS // 2
    return pl.pallas_call(
        _odd_in_kernel,
        grid=(B, nt),
        in_specs=[
            pl.BlockSpec((1, tm, D), lambda b, t: (b, t, 0)),
            pl.BlockSpec((1, N_MOD, D), lambda b, t: (jnp.where(t < n_lat_tiles, b, B), 0, 0)),
            full2(g1), full2(w_inT), full2(q_g), full2(k_g),
            pl.BlockSpec((C_HDIM // 2, tm), lambda b, t: (0, t)),
            pl.BlockSpec((C_HDIM // 2, tm), lambda b, t: (0, t)),
        ],
        out_specs=[
            pl.BlockSpec((1, C_HEADS, tm, HEAD_PAD), lambda b, t: (b, 0, t, 0)),
            pl.BlockSpec((1, np_, tm, HEAD_PAD), lambda b, t: (b, 0, t, 0)),
            pl.BlockSpec((1, np_, tm, HEAD_PAD), lambda b, t: (b, 0, t, 0)),
            pl.BlockSpec((1, tm, D_WIDTH), lambda b, t: (b, t, 0)),
        ],
        out_shape=[
            jax.ShapeDtypeStruct((B, C_HEADS, Lt, HEAD_PAD), BF16),
            jax.ShapeDtypeStruct((B, np_, Lt, HEAD_PAD), BF16),
            jax.ShapeDtypeStruct((B, np_, Lt, HEAD_PAD), BF16),
            jax.ShapeDtypeStruct((B, Lt, D_WIDTH), F32),
        ],
        compiler_params=_cparams(("parallel", "arbitrary")),
        name="odd_in",
    )(x, m, g1, w_inT, q_g, k_g, cosT, sinT)


def _win_attn_kernel(sink_ref, q_ref, k_ref, v_ref, o_ref, *, n_lat_tiles, n_lat):
    pair = pl.program_id(1)
    t = pl.program_id(2)
    nh = q_ref.shape[1]
    tq = q_ref.shape[2]
    span = tq + 2 * C_WINDOW
    lt = k_ref.shape[2]
    q6 = q_ref[0].reshape(nh * tq, HEAD_PAD)
    kc = k_ref[0, 0, n_lat:lt, :]
    vc = v_ref[0, 0, n_lat:lt, :]
    lane = lax.broadcasted_iota(jnp.int32, (tq, HEAD_PAD), 1)

    def finish(o2, ls):
        half = nh // 2
        for j in range(half):
            lo = o2[j * tq:(j + 1) * tq] / ls[j]
            hi = o2[(half + j) * tq:(half + j + 1) * tq] / ls[half + j]
            o_ref[0, :, j * HEAD_PAD:(j + 1) * HEAD_PAD] = jnp.where(lane < C_HDIM, lo, hi).astype(o_ref.dtype)

    @pl.when(t < n_lat_tiles)
    def _():
        start = pl.multiple_of(jnp.maximum(t * tq - C_WINDOW, 0), C_WINDOW)
        kw = k_ref[0, 0, pl.ds(start, span), :]
        vw = v_ref[0, 0, pl.ds(start, span), :]
        s_win = lax.dot_general(q6, kw, NT_DIMS, preferred_element_type=F32)
        s_ctx = lax.dot_general(q6, kc, NT_DIMS, preferred_element_type=F32)
        qpos = t * tq + lax.broadcasted_iota(jnp.int32, (tq, span), 0)
        kpos = start + lax.broadcasted_iota(jnp.int32, (tq, span), 1)
        valid = (jnp.abs(kpos - qpos) <= C_WINDOW) & (kpos < n_lat)
        p_win, p_ctx, ls = [], [], []
        for j in range(nh):
            sk = sink_ref[pair * nh + j]
            sw = jnp.where(valid, s_win[j * tq:(j + 1) * tq], NEG_INF)
            sc = s_ctx[j * tq:(j + 1) * tq]
            mx = jnp.maximum(jnp.maximum(jnp.max(sw, axis=-1, keepdims=True), jnp.max(sc, axis=-1, keepdims=True)), sk)
            pw = jnp.exp(sw - mx)
            pc = jnp.exp(sc - mx)
            ls.append(jnp.sum(pw, axis=-1, keepdims=True) + jnp.sum(pc, axis=-1, keepdims=True) + jnp.exp(sk - mx))
            p_win.append(pw.astype(BF16))
            p_ctx.append(pc.astype(BF16))
        o2 = (jnp.dot(jnp.concatenate(p_win, axis=0), vw, preferred_element_type=F32)
              + jnp.dot(jnp.concatenate(p_ctx, axis=0), vc, preferred_element_type=F32))
        finish(o2, ls)

    @pl.when(t >= n_lat_tiles)
    def _():
        s_ctx = lax.dot_general(q6, kc, NT_DIMS, preferred_element_type=F32)
        p_ctx, ls = [], []
        for j in range(nh):
            sk = sink_ref[pair * nh + j]
            sc = s_ctx[j * tq:(j + 1) * tq]
            mx = jnp.maximum(jnp.max(sc, axis=-1, keepdims=True), sk)
            pc = jnp.exp(sc - mx)
            ls.append(jnp.sum(pc, axis=-1, keepdims=True) + jnp.exp(sk - mx))
            p_ctx.append(pc.astype(BF16))
        finish(jnp.dot(jnp.concatenate(p_ctx, axis=0), vc, preferred_element_type=F32), ls)


def _win_attn(sink, q, k2, v2, n_lat):
    B, H, Lt, _ = q.shape
    tq = TOK_TILE
    npair = k2.shape[1]
    nh = H // npair
    return pl.pallas_call(
        functools.partial(_win_attn_kernel, n_lat_tiles=n_lat // tq, n_lat=n_lat),
        grid=(B, npair, Lt // tq),
        in_specs=[
            pl.BlockSpec(memory_space=pltpu.SMEM),
            pl.BlockSpec((1, nh, tq, HEAD_PAD), lambda b, p, t: (b, p, t, 0)),
            pl.BlockSpec((1, 1, Lt, HEAD_PAD), lambda b, p, t: (b, p, 0, 0)),
            pl.BlockSpec((1, 1, Lt, HEAD_PAD), lambda b, p, t: (b, p, 0, 0)),
        ],
        out_specs=pl.BlockSpec((1, tq, nh * C_HDIM), lambda b, p, t: (b, t, p)),
        out_shape=jax.ShapeDtypeStruct((B, Lt, H * C_HDIM), BF16),
        compiler_params=_cparams(("parallel", "arbitrary", "arbitrary")),
        name="win_attn",
    )(sink, q, k2, v2)


def _pool_lane_tile(z, wp, ps, g0):
    n = z.shape[0]
    row = lax.broadcasted_iota(jnp.int32, z.shape, 0)
    low = lax.broadcasted_iota(jnp.int32, z.shape, 1) < D_GDIM

    def shift_down(a, s):
        return jnp.where(row >= s, pltpu.roll(a, s, 0), 0.0)

    def shift_up(a, s):
        return jnp.where(row < n - s, pltpu.roll(a, n - s, 0), 0.0)

    h_lo, h_hi = D_WINDOWS[g0] // 2, D_WINDOWS[g0 + 1] // 2
    back, fwd, h = z, z, 1
    levels = {1: (z, z)}
    while h < h_hi:
        back = back + shift_down(back, h)
        fwd = fwd + shift_up(fwd, h)
        h *= 2
        levels[h] = (back, fwd)
    back_sel = jnp.where(low, levels[h_lo][0], levels[h_hi][0])
    fwd_sel = jnp.where(low, levels[h_lo][1], levels[h_hi][1])
    win_sum = shift_down(back_sel, 1) + fwd_sel
    half = jnp.where(low, h_lo, h_hi)
    cnt = jnp.minimum(row + half, n) - jnp.maximum(row - half, 0)
    pooled = (win_sum / cnt.astype(F32) - z).astype(BF16)
    return jnp.dot(pooled, wp, preferred_element_type=F32) * ps


def _pool_kernel(pz_ref, wp_ref, ps_ref, o_ref, *, n_lat):
    lt = pz_ref.shape[1]
    for lo, hi in ((0, n_lat), (n_lat, lt)):
        for c in range(D_GROUPS // 2):
            c0, c1 = c * LANES, (c + 1) * LANES
            y = _pool_lane_tile(pz_ref[0, lo:hi, c0:c1], wp_ref[c0:c1, c0:c1], ps_ref[:, c0:c1], 2 * c)
            o_ref[0, lo:hi, c0:c1] = y.astype(o_ref.dtype)


def _pool(pz, wp_bd, p_scale, n_lat):
    B, Lt, W = pz.shape
    return pl.pallas_call(
        functools.partial(_pool_kernel, n_lat=n_lat),
        grid=(B,),
        in_specs=[
            pl.BlockSpec((1, Lt, W), lambda b: (b, 0, 0)),
            pl.BlockSpec(wp_bd.shape, lambda b: (0, 0)),
            pl.BlockSpec(p_scale.shape, lambda b: (0, 0)),
        ],
        out_specs=pl.BlockSpec((1, Lt, W), lambda b: (b, 0, 0)),
        out_shape=jax.ShapeDtypeStruct((B, Lt, W), BF16),
        compiler_params=_cparams(("parallel",)),
        name="pool",
    )(pz, wp_bd, p_scale)


def _out_kernel(x_ref, m_ref, a_ref, b_ref, wa_ref, wb_ref, o_ref):
    y = (jnp.dot(a_ref[0], wa_ref[...], preferred_element_type=F32)
         + jnp.dot(b_ref[0], wb_ref[...], preferred_element_type=F32))
    o_ref[0] = x_ref[0] + m_ref[0][2:3] * y


def _out_proj(x, m, att, br, wa, wb, n_lat_tiles):
    B, Lt, D = x.shape
    tm = TOK_TILE
    return pl.pallas_call(
        _out_kernel,
        grid=(B, Lt // tm),
        in_specs=[
            pl.BlockSpec((1, tm, D), lambda b, t: (b, t, 0)),
            pl.BlockSpec((1, N_MOD, D), lambda b, t: (jnp.where(t < n_lat_tiles, b, B), 0, 0)),
            pl.BlockSpec((1, tm, att.shape[2]), lambda b, t: (b, t, 0)),
            pl.BlockSpec((1, tm, br.shape[2]), lambda b, t: (b, t, 0)),
            pl.BlockSpec(wa.shape, lambda b, t: (0, 0)),
            pl.BlockSpec(wb.shape, lambda b, t: (0, 0)),
        ],
        out_specs=pl.BlockSpec((1, tm, D), lambda b, t: (b, t, 0)),
        out_shape=jax.ShapeDtypeStruct((B, Lt, D), F32),
        compiler_params=_cparams(("parallel", "arbitrary")),
        name="out_proj",
    )(x, m, att, br, wa, wb)


def _ffn_kernel(x_ref, xp_ref, xn_ref, m_ref, g2_ref, wup_ref, cw_ref, cb_ref, wdn_ref, o_ref, u_scr, act_scr,
                *, n_lat, d_ff):
    t = pl.program_id(1)
    tm = x_ref.shape[1]
    lt = pl.num_programs(1) * tm
    halo = FFN_HALO
    rows = tm + 2 * halo
    m = m_ref[...]
    g2 = g2_ref[...]

    def mod_norm_rows(xv, first_row):
        r = first_row + lax.broadcasted_iota(jnp.int32, (xv.shape[0], 1), 0)
        is_lat = r < n_lat
        shift = jnp.where(is_lat, m[0, 3:4], m[1, 3:4])
        scale = jnp.where(is_lat, m[0, 4:5], m[1, 4:5])
        return _mod_norm(xv, g2, shift, scale).astype(BF16)

    row0 = t * tm
    u_scr[0:halo] = mod_norm_rows(xp_ref[0], row0 - halo)
    u_scr[halo:halo + tm] = mod_norm_rows(x_ref[0], row0)
    u_scr[halo + tm:rows] = mod_norm_rows(xn_ref[0], row0 + tm)

    r = row0 - halo + lax.broadcasted_iota(jnp.int32, (rows, 1), 0)
    left_ok = ((r != 0) & (r != n_lat)).astype(F32)
    right_ok = ((r != n_lat - 1) & (r != lt - 1)).astype(F32)

    def conv(h, col0, width):
        cw = cw_ref[:, col0:col0 + width]
        hm = pltpu.roll(h, 1, 0) * left_ok
        hp = pltpu.roll(h, rows - 1, 0) * right_ok
        out = hm * cw[0:1] + h * cw[1:2] + hp * cw[2:3] + cb_ref[:, col0:col0 + width]
        return out[halo:halo + tm]

    u = u_scr[...]
    for c in range(d_ff // FFN_CHUNK):
        c0 = c * FFN_CHUNK
        gate = conv(jnp.dot(u, wup_ref[:, c0:c0 + FFN_CHUNK], preferred_element_type=F32), c0, FFN_CHUNK)
        val = conv(jnp.dot(u, wup_ref[:, d_ff + c0:d_ff + c0 + FFN_CHUNK], preferred_element_type=F32),
                   d_ff + c0, FFN_CHUNK)
        act_scr[:, c0:c0 + FFN_CHUNK] = (gate * jax.nn.sigmoid(gate) * val).astype(BF16)

    y = jnp.dot(act_scr[...], wdn_ref[...], preferred_element_type=F32)
    is_lat = (row0 + lax.broadcasted_iota(jnp.int32, (tm, 1), 0)) < n_lat
    o_ref[0] = x_ref[0] + jnp.where(is_lat, m[0, 5:6], m[1, 5:6]) * y


def _ffn_tile(lt):
    best = FFN_HALO
    for cand in range(FFN_HALO, 577, FFN_HALO):
        if lt % cand == 0:
            best = cand
    return best


def _ffn(x, m2, g2, w_up, conv_w, conv_b, w_down, n_lat):
    B, Lt, D = x.shape
    d_ff = w_down.shape[0]
    tm = _ffn_tile(Lt)
    hb = tm // FFN_HALO
    nhb = Lt // FFN_HALO
    once = pl.Buffered(1)
    return pl.pallas_call(
        functools.partial(_ffn_kernel, n_lat=n_lat, d_ff=d_ff),
        grid=(B, Lt // tm),
        in_specs=[
            pl.BlockSpec((1, tm, D), lambda b, t: (b, t, 0)),
            pl.BlockSpec((1, FFN_HALO, D), lambda b, t: (b, jnp.maximum(t * hb - 1, 0), 0)),
            pl.BlockSpec((1, FFN_HALO, D), lambda b, t: (b, jnp.minimum((t + 1) * hb, nhb - 1), 0)),
            pl.BlockSpec((None, 2, N_MOD, D), lambda b, t: (b, 0, 0, 0)),
            pl.BlockSpec(g2.shape, lambda b, t: (0, 0)),
            pl.BlockSpec(w_up.shape, lambda b, t: (0, 0), pipeline_mode=once),
            pl.BlockSpec(conv_w.shape, lambda b, t: (0, 0)),
            pl.BlockSpec(conv_b.shape, lambda b, t: (0, 0)),
            pl.BlockSpec(w_down.shape, lambda b, t: (0, 0), pipeline_mode=once),
        ],
        out_specs=pl.BlockSpec((1, tm, D), lambda b, t: (b, t, 0)),
        out_shape=jax.ShapeDtypeStruct((B, Lt, D), F32),
        scratch_shapes=[pltpu.VMEM((tm + 2 * FFN_HALO, D), BF16), pltpu.VMEM((tm, d_ff), BF16)],
        compiler_params=_cparams(("parallel", "arbitrary")),
        name="conv_ffn",
    )(x, x, x, m2, g2, w_up, conv_w, conv_b, w_down)


def _rope_tables_T(rows, rot_dim, n_ctx):
    n_freq = rot_dim // 4
    inv = ROPE_BASE ** (-jnp.arange(n_freq, dtype=F32) / n_freq)
    row = jnp.broadcast_to(jnp.arange(rows, dtype=F32)[:, None], (rows, GRID_W)).reshape(-1)
    col = jnp.broadcast_to(jnp.arange(GRID_W, dtype=F32)[None, :], (rows, GRID_W)).reshape(-1)
    ang = jnp.concatenate([row[:, None] * inv, col[:, None] * inv], axis=-1)
    cosT = jnp.concatenate([jnp.cos(ang).T, jnp.ones((rot_dim // 2, n_ctx), F32)], axis=1)
    sinT = jnp.concatenate([jnp.sin(ang).T, jnp.zeros((rot_dim // 2, n_ctx), F32)], axis=1)
    return cosT, sinT


def _dft_tables(n):
    idx = (jnp.arange(n, dtype=jnp.int32)[:, None] * jnp.arange(n, dtype=jnp.int32)[None, :]) % n
    ang = idx.astype(F32) * (2.0 * math.pi / n)
    return jnp.cos(ang), jnp.sin(ang)


def _block_diag(w):
    g, a, b = w.shape
    eye = jnp.eye(g, dtype=w.dtype)
    return (eye[:, None, :, None] * w[:, :, None, :]).reshape(g * a, g * b)


def _col(v):
    return v.reshape(-1, 1).astype(F32)


def kernel(x, c, ctx, c_ctx, mod_w, mod_b, norm1_g, norm2_g, mla_w_in, mla_cq_g, mla_ckv_g, mla_w_uq, mla_w_ukv,
           mla_q_g, mla_k_g, fnet_w, even_w_out, win_w_in, win_q_g, win_k_g, win_sink, pool_w, pool_scale, odd_w_out,
           ffn_up, ffn_conv_w, ffn_conv_b, ffn_down):
    B, L, D = x.shape
    C = ctx.shape[1]
    depth = mod_w.shape[0]
    assert L % TOK_TILE == 0 and C == TOK_TILE and B % FNET_BGRP == 0
    n_lat_tiles = L // TOK_TILE
    rows = L // GRID_W

    pad = (-(B + 1)) % 8
    cc = jnp.concatenate([c, c_ctx[None, :], jnp.zeros((pad, D), F32)], axis=0)
    mods = _modulation(cc, mod_w, mod_b).reshape(depth, B + 1 + pad, N_MOD, D)

    cos_a, sin_a = _rope_tables_T(rows, A_ROPE, C)
    cos_w, sin_w = _rope_tables_T(rows, C_HDIM, C)
    c_lat, s_lat = (t.astype(BF16) for t in _dft_tables(L))
    c_ctx_t, s_ctx_t = (t.astype(BF16) for t in _dft_tables(C))
    cc64, ss64 = _dft_tables(B_GDIM)
    eye_g = jnp.eye(B_GROUPS, dtype=F32)
    cs_bd = jnp.concatenate([jnp.kron(eye_g, cc64), -jnp.kron(eye_g, ss64)], axis=0).astype(BF16)

    nh = C_HEADS // (C_KV_HEADS // 2)
    head_order = [p * nh + e * (nh // 2) + j for p in range(C_KV_HEADS // 2) for j in range(nh // 2) for e in range(2)]
    att_perm = jnp.asarray([h * C_HDIM + d for h in head_order for d in range(C_HDIM)], jnp.int32)

    h = jnp.concatenate([x, ctx], axis=1)
    for i in range(depth):
        j = i // 2
        m = mods[i]
        g1 = norm1_g[i][None, :]
        if i % 2 == 0:
            q, kT, vT, z = _even_in(
                h, m, g1, mla_w_in[j].T.astype(BF16), _col(mla_cq_g[j]), mla_w_uq[j].T.astype(BF16),
                _col(mla_q_g[j]), _col(mla_ckv_g[j]), mla_w_ukv[j].T.astype(BF16), _col(mla_k_g[j]),
                cos_a, sin_a, n_lat_tiles)
            att = _mla_attn(q, kT, vT, L)
            wf_bd = _block_diag(fnet_w[j]).astype(BF16)
            br = _fnet(z, c_lat, s_lat, cs_bd, wf_bd, 0, L + C)
            br = _fnet(z, c_ctx_t, s_ctx_t, cs_bd, wf_bd, L, L + C, prev_out=br)
            w_out = even_w_out[j].astype(BF16)
            wa, wb = w_out[:A_HEADS * A_V], w_out[A_HEADS * A_V:]
        else:
            q, k2, v2, pz = _odd_in(h, m, g1, win_w_in[j].T.astype(BF16), _col(win_q_g[j]), _col(win_k_g[j]),
                                    cos_w, sin_w, n_lat_tiles)
            att = _win_attn(win_sink[j].reshape(-1), q, k2, v2, L)
            br = _pool(pz, _block_diag(pool_w[j]).astype(BF16), pool_scale[j][None, :], L)
            w_out = odd_w_out[j].astype(BF16)
            wa, wb = w_out[:C_QW][att_perm], w_out[C_QW:]
        h = _out_proj(h, m, att, br, wa, wb, n_lat_tiles)
        m2 = jnp.stack([m[:B], jnp.broadcast_to(m[B][None], (B, N_MOD, D))], axis=1)
        h = _ffn(h, m2, norm2_g[i][None, :], ffn_up[i].astype(BF16), ffn_conv_w[i], ffn_conv_b[i][None, :],
                 ffn_down[i].astype(BF16), L)
    return h[:, :L]
```

```python
import functools
import math

import jax
import jax.numpy as jnp
from jax import lax
from jax.experimental import pallas as pl
from jax.experimental.pallas import tpu as pltpu

F32 = jnp.float32
BF16 = jnp.bfloat16

GRID_W = 64
N_MOD = 6
EPS = 1e-6
ROPE_BASE = 10000.0
NEG_INF = -1e30
A_HEADS = 12
A_NOPE = 64
A_ROPE = 32
A_QK = A_NOPE + A_ROPE
A_V = 64
A_Q_LORA = 256
A_KV_LORA = 128
A_IN = A_Q_LORA + A_KV_LORA + A_ROPE
B_GROUPS = 4
B_GDIM = 64
B_WIDTH = B_GROUPS * B_GDIM
C_HEADS = 12
C_KV_HEADS = 4
C_GROUP = C_HEADS // C_KV_HEADS
C_HDIM = 64
C_WINDOW = 128
C_QW = C_HEADS * C_HDIM
C_KW = C_KV_HEADS * C_HDIM
D_GROUPS = 4
D_GDIM = 64
D_WIDTH = D_GROUPS * D_GDIM
D_WINDOWS = (2, 4, 8, 16)

LANES = 128
BF16_ROWS = 16
HEAD_PAD = LANES
TOK_TILE = 256
FFN_HALO = 8
FFN_CHUNK = 256
FFN_LAT_TILE = 512
FNET_BGRP = 4
MLA_Q_TILE = 512
MLA_KEY_CHUNK = 256
LOG2_E = math.log2(math.e)
WIN_KEY_CHUNK = 256
VMEM_LIMIT = 56 * 1024 * 1024

NT_DIMS = (((1,), (1,)), ((), ()))


def _cparams(sem):
    return pltpu.CompilerParams(dimension_semantics=sem, vmem_limit_bytes=VMEM_LIMIT)


def _mod_norm(x, g, shift, scale):
    ms = jnp.mean(x * x, axis=-1, keepdims=True)
    y = x * lax.rsqrt(ms + EPS) * g
    return y * (1.0 + scale) + shift


def _rms_rows(xT, g_col):
    ms = jnp.mean(xT * xT, axis=0, keepdims=True)
    return xT * lax.rsqrt(ms + EPS) * g_col


def _rope_rows(x1, x2, cos, sin):
    return x1 * cos - x2 * sin, x1 * sin + x2 * cos


def _mod_kernel(cc_ref, w_ref, b_ref, o_ref):
    cc = cc_ref[...]
    a = (cc * jax.nn.sigmoid(cc)).astype(BF16)
    w = w_ref[0].astype(BF16)
    o_ref[0] = jnp.dot(a, w, preferred_element_type=F32) + b_ref[0]


def _modulation(cc, mod_w, mod_b):
    depth, d, n = mod_w.shape
    rows = cc.shape[0]
    tn = 1024
    return pl.pallas_call(
        _mod_kernel,
        grid=(depth, n // tn),
        in_specs=[
            pl.BlockSpec((rows, d), lambda i, j: (0, 0)),
            pl.BlockSpec((1, d, tn), lambda i, j: (i, 0, j)),
            pl.BlockSpec((1, 1, tn), lambda i, j: (i, 0, j)),
        ],
        out_specs=pl.BlockSpec((1, rows, tn), lambda i, j: (i, 0, j)),
        out_shape=jax.ShapeDtypeStruct((depth, rows, n), F32),
        compiler_params=_cparams(("arbitrary", "arbitrary")),
        name="modulation",
    )(cc, mod_w, mod_b.reshape(depth, 1, n))


def _even_in_kernel(x_ref, m_ref, g1_ref, w_in_ref, cqg_ref, wuq_ref, qg_ref, ckvg_ref, wukv_ref, kg_ref,
                    cos_ref, sin_ref, q_ref, k_ref, v_ref, z_ref):
    tm = x_ref.shape[1]
    m = m_ref[0]
    u = _mod_norm(x_ref[0], g1_ref[...], m[0:1], m[1:2]).astype(BF16)
    hT = lax.dot_general(w_in_ref[...], u, NT_DIMS, preferred_element_type=F32)
    cos = cos_ref[...]
    sin = sin_ref[...]
    scale = A_QK ** -0.5 * LOG2_E
    half = A_ROPE // 2
    zpad = jnp.zeros((HEAD_PAD - A_QK, tm), F32)
    ones_pad = (lax.broadcasted_iota(jnp.int32, (HEAD_PAD - A_V, tm), 0) < BF16_ROWS).astype(F32)

    z_ref[0] = hT[A_IN:A_IN + B_WIDTH].T.astype(BF16)

    cqn = _rms_rows(hT[0:A_Q_LORA], cqg_ref[...]).astype(BF16)
    qT = jnp.dot(wuq_ref[...], cqn, preferred_element_type=F32)
    qg = qg_ref[...] * scale
    for h in range(A_HEADS):
        qn = _rms_rows(qT[h * A_QK:(h + 1) * A_QK], qg)
        o1, o2 = _rope_rows(qn[A_NOPE:A_NOPE + half], qn[A_NOPE + half:A_QK], cos, sin)
        full = jnp.concatenate([qn[0:A_NOPE], o1, o2, zpad], axis=0)
        q_ref[0, h] = full.T.astype(BF16)

    ckvn = _rms_rows(hT[A_Q_LORA:A_Q_LORA + A_KV_LORA], ckvg_ref[...]).astype(BF16)
    kvT = jnp.dot(wukv_ref[...], ckvn, preferred_element_type=F32)
    kr = hT[A_Q_LORA + A_KV_LORA:A_IN]
    kr_ss = jnp.sum(kr * kr, axis=0, keepdims=True)
    kg = kg_ref[...]
    hw = A_NOPE + A_V
    for h in range(A_HEADS):
        kn = kvT[h * hw:h * hw + A_NOPE]
        r = lax.rsqrt((jnp.sum(kn * kn, axis=0, keepdims=True) + kr_ss) * (1.0 / A_QK) + EPS)
        knn = kn * r * kg[0:A_NOPE]
        krn = kr * r * kg[A_NOPE:A_QK]
        o1, o2 = _rope_rows(krn[0:half], krn[half:A_ROPE], cos, sin)
        k_ref[0, h] = jnp.concatenate([knn, o1, o2, zpad], axis=0).astype(BF16)
        vh = kvT[h * hw + A_NOPE:(h + 1) * hw]
        v_ref[0, h] = jnp.concatenate([vh, ones_pad] if h % 2 == 0 else [ones_pad, vh], axis=0).astype(BF16)


def _even_in(x, m, g1, w_inT, cq_g, w_uqT, q_g, ckv_g, w_ukvT, k_g, cosT, sinT, n_lat_tiles):
    B, Lt, D = x.shape
    tm = TOK_TILE
    nt = Lt // tm
    full2 = lambda a: pl.BlockSpec(a.shape, lambda b, t: (0, 0))
    return pl.pallas_call(
        _even_in_kernel,
        grid=(B, nt),
        in_specs=[
            pl.BlockSpec((1, tm, D), lambda b, t: (b, t, 0)),
            pl.BlockSpec((1, N_MOD, D), lambda b, t: (jnp.where(t < n_lat_tiles, b, B), 0, 0)),
            full2(g1), full2(w_inT), full2(cq_g), full2(w_uqT), full2(q_g), full2(ckv_g), full2(w_ukvT), full2(k_g),
            pl.BlockSpec((A_ROPE // 2, tm), lambda b, t: (0, t)),
            pl.BlockSpec((A_ROPE // 2, tm), lambda b, t: (0, t)),
        ],
        out_specs=[
            pl.BlockSpec((1, A_HEADS, tm, HEAD_PAD), lambda b, t: (b, 0, t, 0)),
            pl.BlockSpec((1, A_HEADS, HEAD_PAD, tm), lambda b, t: (b, 0, 0, t)),
            pl.BlockSpec((1, A_HEADS, HEAD_PAD, tm), lambda b, t: (b, 0, 0, t)),
            pl.BlockSpec((1, tm, B_WIDTH), lambda b, t: (b, t, 0)),
        ],
        out_shape=[
            jax.ShapeDtypeStruct((B, A_HEADS, Lt, HEAD_PAD), BF16),
            jax.ShapeDtypeStruct((B, A_HEADS, HEAD_PAD, Lt), BF16),
            jax.ShapeDtypeStruct((B, A_HEADS, HEAD_PAD, Lt), BF16),
            jax.ShapeDtypeStruct((B, Lt, B_WIDTH), BF16),
        ],
        compiler_params=_cparams(("parallel", "arbitrary")),
        name="even_in",
    )(x, m, g1, w_inT, cq_g, w_uqT, q_g, ckv_g, w_ukvT, k_g, cosT, sinT)


def _mla_attn_kernel(*refs, aliased):
    q_ref, k_ref, v_ref, o_ref = refs[1:] if aliased else refs
    tq = q_ref.shape[2]
    nk = k_ref.shape[3]
    lane = lax.broadcasted_iota(jnp.int32, (tq, HEAD_PAD), 1)
    qs = [q_ref[0, j] for j in range(2)]
    m = [None, None]
    acc = [None, None]
    for k0 in range(0, nk, MLA_KEY_CHUNK):
        k1 = min(k0 + MLA_KEY_CHUNK, nk)
        for j in range(2):
            s = jnp.dot(qs[j], k_ref[0, j, :, k0:k1], preferred_element_type=F32)
            mx = jnp.max(s, axis=-1, keepdims=True)
            m_new = mx if m[j] is None else jnp.maximum(m[j], mx)
            p = jnp.exp2(s - m_new).astype(BF16)
            pv = lax.dot_general(p, v_ref[0, j, :, k0:k1], NT_DIMS, preferred_element_type=F32)
            acc[j] = pv if m[j] is None else jnp.exp2(m[j] - m_new) * acc[j] + pv
            m[j] = m_new
    o0 = acc[0] / acc[0][:, A_V:A_V + 1]
    o1 = acc[1] / acc[1][:, 0:1]
    o_ref[0] = jnp.where(lane < A_V, o0, o1).astype(o_ref.dtype)


def _mla_attn(q, kT, vT, q_row0, n_q, tq, k_col0, n_k, prev_out=None):
    B, H, Lt, _ = q.shape
    assert q_row0 % tq == 0 and n_q % tq == 0 and k_col0 % n_k == 0
    qb, kb = q_row0 // tq, k_col0 // n_k
    aliased = prev_out is not None
    in_specs = [
        pl.BlockSpec((1, 2, tq, HEAD_PAD), lambda b, h, t: (b, h, qb + t, 0)),
        pl.BlockSpec((1, 2, HEAD_PAD, n_k), lambda b, h, t: (b, h, 0, kb)),
        pl.BlockSpec((1, 2, HEAD_PAD, n_k), lambda b, h, t: (b, h, 0, kb)),
    ]
    args = [q, kT, vT]
    if aliased:
        in_specs = [pl.BlockSpec(memory_space=pl.ANY)] + in_specs
        args = [prev_out] + args
    return pl.pallas_call(
        functools.partial(_mla_attn_kernel, aliased=aliased),
        grid=(B, H // 2, n_q // tq),
        in_specs=in_specs,
        out_specs=pl.BlockSpec((1, tq, 2 * A_V), lambda b, h, t: (b, qb + t, h)),
        out_shape=jax.ShapeDtypeStruct((B, Lt, H * A_V), BF16),
        input_output_aliases={0: 0} if aliased else {},
        compiler_params=_cparams(("parallel", "arbitrary", "arbitrary")),
        name="mla_attn",
    )(*args)


def _fnet_kernel(*refs, norm, aliased):
    if aliased:
        refs = refs[1:]
    c_ref, s_ref, z_ref, cs_ref, wf_ref, o_ref, p_acc, q_acc = refs
    k = pl.program_id(2)

    @pl.when(k == 0)
    def _():
        p_acc[...] = jnp.zeros_like(p_acc)
        q_acc[...] = jnp.zeros_like(q_acc)

    c = c_ref[...]
    s = s_ref[...]
    for j in range(z_ref.shape[0]):
        z = z_ref[j]
        p_acc[j] += jnp.dot(c, z, preferred_element_type=F32)
        q_acc[j] += jnp.dot(s, z, preferred_element_type=F32)

    @pl.when(k == pl.num_programs(2) - 1)
    def _():
        for j in range(z_ref.shape[0]):
            pq = jnp.concatenate([p_acc[j], q_acc[j]], axis=1).astype(BF16)
            y = jnp.dot(pq, cs_ref[...], preferred_element_type=F32) * norm
            o_ref[j] = jnp.dot(y.astype(BF16), wf_ref[...], preferred_element_type=F32).astype(o_ref.dtype)


def _fnet(z, c_tab, s_tab, cs_bd, wf_bd, row0, out_rows, prev_out=None):
    B = z.shape[0]
    n = c_tab.shape[0]
    tm = min(n, 1024)
    tk = min(n, 1024)
    g = FNET_BGRP
    norm = 1.0 / math.sqrt(n * B_GDIM)
    aliased = prev_out is not None
    in_specs = [
        pl.BlockSpec((tm, tk), lambda b, i, k: (i, k)),
        pl.BlockSpec((tm, tk), lambda b, i, k: (i, k)),
        pl.BlockSpec((g, tk, B_WIDTH), lambda b, i, k: (b, row0 // tk + k, 0)),
        pl.BlockSpec(cs_bd.shape, lambda b, i, k: (0, 0)),
        pl.BlockSpec(wf_bd.shape, lambda b, i, k: (0, 0)),
    ]
    args = [c_tab, s_tab, z, cs_bd, wf_bd]
    if aliased:
        in_specs = [pl.BlockSpec(memory_space=pl.ANY)] + in_specs
        args = [prev_out] + args
    return pl.pallas_call(
        functools.partial(_fnet_kernel, norm=norm, aliased=aliased),
        grid=(B // g, n // tm, n // tk),
        in_specs=in_specs,
        out_specs=pl.BlockSpec((g, tm, B_WIDTH), lambda b, i, k: (b, row0 // tm + i, 0)),
        out_shape=jax.ShapeDtypeStruct((B, out_rows, B_WIDTH), BF16),
        scratch_shapes=[pltpu.VMEM((g, tm, B_WIDTH), F32), pltpu.VMEM((g, tm, B_WIDTH), F32)],
        input_output_aliases={0: 0} if aliased else {},
        compiler_params=_cparams(("parallel", "arbitrary", "arbitrary")),
        name="fnet",
    )(*args)


def _odd_in_kernel(x_ref, m_ref, g1_ref, w_in_ref, qg_ref, kg_ref, cos_ref, sin_ref, q_ref, k_ref, v_ref, pz_ref):
    tm = x_ref.shape[1]
    m = m_ref[0]
    u = _mod_norm(x_ref[0], g1_ref[...], m[0:1], m[1:2]).astype(BF16)
    hT = lax.dot_general(w_in_ref[...], u, NT_DIMS, preferred_element_type=F32)
    cos = cos_ref[...]
    sin = sin_ref[...]
    half = C_HDIM // 2
    zeros = jnp.zeros((C_HDIM, tm), F32)

    def norm_rope(xT, g_col):
        xn = _rms_rows(xT, g_col)
        o1, o2 = _rope_rows(xn[0:half], xn[half:C_HDIM], cos, sin)
        return jnp.concatenate([o1, o2], axis=0)

    qg = qg_ref[...] * (C_HDIM ** -0.5 * LOG2_E)
    for h in range(C_HEADS):
        qh = norm_rope(hT[h * C_HDIM:(h + 1) * C_HDIM], qg)
        parts = [qh, zeros] if (h // C_GROUP) % 2 == 0 else [zeros, qh]
        q_ref[0, h] = jnp.concatenate(parts, axis=0).T.astype(BF16)

    kg = kg_ref[...]
    for p in range(C_KV_HEADS // 2):
        ks = [norm_rope(hT[C_QW + (2 * p + e) * C_HDIM:C_QW + (2 * p + e + 1) * C_HDIM], kg) for e in range(2)]
        k_ref[0, p] = jnp.concatenate(ks, axis=0).T.astype(BF16)
        v0 = C_QW + C_KW + 2 * p * C_HDIM
        v_ref[0, p] = hT[v0:v0 + 2 * C_HDIM].T.astype(BF16)

    pz_ref[0] = hT[C_QW + 2 * C_KW:C_QW + 2 * C_KW + D_WIDTH].T


def _odd_in(x, m, g1, w_inT, q_g, k_g, cosT, sinT, n_lat_tiles):
    B, Lt, D = x.shape
    tm = TOK_TILE
    nt = Lt // tm
    full2 = lambda a: pl.BlockSpec(a.shape, lambda b, t: (0, 0))
    np_ = C_KV_HEADS // 2
    return pl.pallas_call(
        _odd_in_kernel,
        grid=(B, nt),
        in_specs=[
            pl.BlockSpec((1, tm, D), lambda b, t: (b, t, 0)),
            pl.BlockSpec((1, N_MOD, D), lambda b, t: (jnp.where(t < n_lat_tiles, b, B), 0, 0)),
            full2(g1), full2(w_inT), full2(q_g), full2(k_g),
            pl.BlockSpec((C_HDIM // 2, tm), lambda b, t: (0, t)),
            pl.BlockSpec((C_HDIM // 2, tm), lambda b, t: (0, t)),
        ],
        out_specs=[
            pl.BlockSpec((1, C_HEADS, tm, HEAD_PAD), lambda b, t: (b, 0, t, 0)),
            pl.BlockSpec((1, np_, tm, HEAD_PAD), lambda b, t: (b, 0, t, 0)),
            pl.BlockSpec((1, np_, tm, HEAD_PAD), lambda b, t: (b, 0, t, 0)),
            pl.BlockSpec((1, tm, D_WIDTH), lambda b, t: (b, t, 0)),
        ],
        out_shape=[
            jax.ShapeDtypeStruct((B, C_HEADS, Lt, HEAD_PAD), BF16),
            jax.ShapeDtypeStruct((B, np_, Lt, HEAD_PAD), BF16),
            jax.ShapeDtypeStruct((B, np_, Lt, HEAD_PAD), BF16),
            jax.ShapeDtypeStruct((B, Lt, D_WIDTH), F32),
        ],
        compiler_params=_cparams(("parallel", "arbitrary")),
        name="odd_in",
    )(x, m, g1, w_inT, q_g, k_g, cosT, sinT)


def _win_attn_kernel(sink_ref, q_ref, k_ref, v_ref, o_ref, *, n_lat_tiles, n_lat):
    pair = pl.program_id(1)
    t = pl.program_id(2)
    nh = q_ref.shape[1]
    tq = q_ref.shape[2]
    lt = k_ref.shape[2]
    half = nh // 2
    ck = WIN_KEY_CHUNK
    row = lax.broadcasted_iota(jnp.int32, (2 * tq, 1), 0)
    lane = lax.broadcasted_iota(jnp.int32, (tq, HEAD_PAD), 1)

    def run(chunks):
        for j in range(half):
            qg = jnp.concatenate([q_ref[0, j], q_ref[0, half + j]], axis=0)
            sk_lo = sink_ref[pair * nh + j] * LOG2_E
            sk_hi = sink_ref[pair * nh + half + j] * LOG2_E
            m = jnp.where(row < tq, sk_lo, sk_hi)
            l = None
            acc = None
            for kb, vb, valid in chunks:
                s = lax.dot_general(qg, kb, NT_DIMS, preferred_element_type=F32)
                if valid is not None:
                    s = jnp.where(valid, s, NEG_INF)
                m_new = jnp.maximum(m, jnp.max(s, axis=-1, keepdims=True))
                alpha = jnp.exp2(m - m_new)
                p = jnp.exp2(s - m_new)
                psum = jnp.sum(p, axis=-1, keepdims=True)
                pv = jnp.dot(p.astype(BF16), vb, preferred_element_type=F32)
                l = alpha + psum if l is None else alpha * l + psum
                acc = pv if acc is None else alpha * acc + pv
                m = m_new
            o = acc / l
            o_ref[0, :, j * HEAD_PAD:(j + 1) * HEAD_PAD] = jnp.where(lane < C_HDIM, o[0:tq], o[tq:2 * tq]).astype(o_ref.dtype)

    kc = k_ref[0, 0, n_lat:lt, :]
    vc = v_ref[0, 0, n_lat:lt, :]

    @pl.when(t < n_lat_tiles)
    def _():
        start = pl.multiple_of(jnp.maximum(t * tq - C_WINDOW, 0), C_WINDOW)
        qpos = t * tq + (lax.broadcasted_iota(jnp.int32, (2 * tq, ck), 0) & (tq - 1))
        chunks = []
        for c in range((tq + 2 * C_WINDOW) // ck):
            kpos = start + c * ck + lax.broadcasted_iota(jnp.int32, (2 * tq, ck), 1)
            valid = (jnp.abs(kpos - qpos) <= C_WINDOW) & (kpos < n_lat)
            chunks.append((k_ref[0, 0, pl.ds(start + c * ck, ck), :], v_ref[0, 0, pl.ds(start + c * ck, ck), :], valid))
        chunks.append((kc, vc, None))
        run(chunks)

    @pl.when(t >= n_lat_tiles)
    def _():
        run([(kc, vc, None)])


def _win_attn(sink, q, k2, v2, n_lat):
    B, H, Lt, _ = q.shape
    tq = TOK_TILE
    npair = k2.shape[1]
    nh = H // npair
    return pl.pallas_call(
        functools.partial(_win_attn_kernel, n_lat_tiles=n_lat // tq, n_lat=n_lat),
        grid=(B, npair, Lt // tq),
        in_specs=[
            pl.BlockSpec(memory_space=pltpu.SMEM),
            pl.BlockSpec((1, nh, tq, HEAD_PAD), lambda b, p, t: (b, p, t, 0)),
            pl.BlockSpec((1, 1, Lt, HEAD_PAD), lambda b, p, t: (b, p, 0, 0)),
            pl.BlockSpec((1, 1, Lt, HEAD_PAD), lambda b, p, t: (b, p, 0, 0)),
        ],
        out_specs=pl.BlockSpec((1, tq, nh * C_HDIM), lambda b, p, t: (b, t, p)),
        out_shape=jax.ShapeDtypeStruct((B, Lt, H * C_HDIM), BF16),
        compiler_params=_cparams(("parallel", "arbitrary", "arbitrary")),
        name="win_attn",
    )(sink, q, k2, v2)


def _pool_lane_tile(z, wp, ps, g0):
    n = z.shape[0]
    row = lax.broadcasted_iota(jnp.int32, z.shape, 0)
    low = lax.broadcasted_iota(jnp.int32, z.shape, 1) < D_GDIM

    def shift_down(a, s):
        return jnp.where(row >= s, pltpu.roll(a, s, 0), 0.0)

    def shift_up(a, s):
        return jnp.where(row < n - s, pltpu.roll(a, n - s, 0), 0.0)

    h_lo, h_hi = D_WINDOWS[g0] // 2, D_WINDOWS[g0 + 1] // 2
    back, fwd, h = z, z, 1
    levels = {1: (z, z)}
    while h < h_hi:
        back = back + shift_down(back, h)
        fwd = fwd + shift_up(fwd, h)
        h *= 2
        levels[h] = (back, fwd)
    back_sel = jnp.where(low, levels[h_lo][0], levels[h_hi][0])
    fwd_sel = jnp.where(low, levels[h_lo][1], levels[h_hi][1])
    win_sum = shift_down(back_sel, 1) + fwd_sel
    half = jnp.where(low, h_lo, h_hi)
    cnt = jnp.minimum(row + half, n) - jnp.maximum(row - half, 0)
    pooled = (win_sum / cnt.astype(F32) - z).astype(BF16)
    return jnp.dot(pooled, wp, preferred_element_type=F32) * ps


def _pool_kernel(pz_ref, wp_ref, ps_ref, o_ref, *, n_lat):
    lt = pz_ref.shape[1]
    for lo, hi in ((0, n_lat), (n_lat, lt)):
        for c in range(D_GROUPS // 2):
            c0, c1 = c * LANES, (c + 1) * LANES
            y = _pool_lane_tile(pz_ref[0, lo:hi, c0:c1], wp_ref[c0:c1, c0:c1], ps_ref[:, c0:c1], 2 * c)
            o_ref[0, lo:hi, c0:c1] = y.astype(o_ref.dtype)


def _pool(pz, wp_bd, p_scale, n_lat):
    B, Lt, W = pz.shape
    return pl.pallas_call(
        functools.partial(_pool_kernel, n_lat=n_lat),
        grid=(B,),
        in_specs=[
            pl.BlockSpec((1, Lt, W), lambda b: (b, 0, 0)),
            pl.BlockSpec(wp_bd.shape, lambda b: (0, 0)),
            pl.BlockSpec(p_scale.shape, lambda b: (0, 0)),
        ],
        out_specs=pl.BlockSpec((1, Lt, W), lambda b: (b, 0, 0)),
        out_shape=jax.ShapeDtypeStruct((B, Lt, W), BF16),
        compiler_params=_cparams(("parallel",)),
        name="pool",
    )(pz, wp_bd, p_scale)


def _out_kernel(x_ref, m_ref, a_ref, b_ref, wa_ref, wb_ref, o_ref):
    y = (jnp.dot(a_ref[0], wa_ref[...], preferred_element_type=F32)
         + jnp.dot(b_ref[0], wb_ref[...], preferred_element_type=F32))
    o_ref[0] = x_ref[0] + m_ref[0][2:3] * y


def _out_proj(x, m, att, br, wa, wb, n_lat_tiles):
    B, Lt, D = x.shape
    tm = TOK_TILE
    return pl.pallas_call(
        _out_kernel,
        grid=(B, Lt // tm),
        in_specs=[
            pl.BlockSpec((1, tm, D), lambda b, t: (b, t, 0)),
            pl.BlockSpec((1, N_MOD, D), lambda b, t: (jnp.where(t < n_lat_tiles, b, B), 0, 0)),
            pl.BlockSpec((1, tm, att.shape[2]), lambda b, t: (b, t, 0)),
            pl.BlockSpec((1, tm, br.shape[2]), lambda b, t: (b, t, 0)),
            pl.BlockSpec(wa.shape, lambda b, t: (0, 0)),
            pl.BlockSpec(wb.shape, lambda b, t: (0, 0)),
        ],
        out_specs=pl.BlockSpec((1, tm, D), lambda b, t: (b, t, 0)),
        out_shape=jax.ShapeDtypeStruct((B, Lt, D), F32),
        compiler_params=_cparams(("parallel", "arbitrary")),
        name="out_proj",
    )(x, m, att, br, wa, wb)


def _ffn_kernel(*refs, d_ff, aliased):
    if aliased:
        refs = refs[1:]
    x_ref, xp_ref, xn_ref, m_ref, g2_ref, wup_ref, cw_ref, cb_ref, wdn_ref, o_ref, u_scr, act_scr = refs
    t = pl.program_id(1)
    tm = x_ref.shape[1]
    rows = tm + 2 * FFN_HALO
    m = m_ref[0]
    g2 = g2_ref[...]

    def mod_norm_rows(xv):
        return _mod_norm(xv, g2, m[3:4], m[4:5])

    u_scr[0:tm] = mod_norm_rows(x_ref[0]).astype(BF16)
    u_next = jnp.where(t == pl.num_programs(1) - 1, 0.0, mod_norm_rows(xn_ref[0]))
    u_prev = jnp.where(t == 0, 0.0, mod_norm_rows(xp_ref[0]))
    u_scr[tm:rows] = jnp.concatenate([u_next, u_prev], axis=0).astype(BF16)

    def conv(h, col0, width):
        cw = cw_ref[:, col0:col0 + width]
        out = (pltpu.roll(h, 1, 0) * cw[0:1] + h * cw[1:2] + pltpu.roll(h, rows - 1, 0) * cw[2:3]
               + cb_ref[:, col0:col0 + width])
        return out[0:tm]

    u = u_scr[...]
    for c in range(d_ff // FFN_CHUNK):
        c0 = c * FFN_CHUNK
        gate = conv(jnp.dot(u, wup_ref[:, c0:c0 + FFN_CHUNK], preferred_element_type=F32), c0, FFN_CHUNK)
        val = conv(jnp.dot(u, wup_ref[:, d_ff + c0:d_ff + c0 + FFN_CHUNK], preferred_element_type=F32),
                   d_ff + c0, FFN_CHUNK)
        act_scr[:, c0:c0 + FFN_CHUNK] = (gate * jax.nn.sigmoid(gate) * val).astype(BF16)

    y = jnp.dot(act_scr[...], wdn_ref[...], preferred_element_type=F32)
    o_ref[0] = x_ref[0] + m[5:6] * y


def _ffn(x, m, g2, w_up, conv_w, conv_b, w_down, row0, n_rows, tm, ctx_mod, out_rows, prev_out=None):
    B, Lt, D = x.shape
    d_ff = w_down.shape[0]
    assert row0 % tm == 0 and n_rows % tm == 0 and tm % BF16_ROWS == 0 and 2 * FFN_HALO == BF16_ROWS
    t0 = row0 // tm
    hb = tm // FFN_HALO
    h0 = row0 // FFN_HALO
    nhb = Lt // FFN_HALO
    once = pl.Buffered(1)
    aliased = prev_out is not None
    in_specs = [
        pl.BlockSpec((1, tm, D), lambda b, t: (b, t0 + t, 0)),
        pl.BlockSpec((1, FFN_HALO, D), lambda b, t: (b, jnp.maximum(h0 + t * hb - 1, 0), 0)),
        pl.BlockSpec((1, FFN_HALO, D), lambda b, t: (b, jnp.minimum(h0 + (t + 1) * hb, nhb - 1), 0)),
        pl.BlockSpec((1, N_MOD, D), (lambda b, t: (B, 0, 0)) if ctx_mod else (lambda b, t: (b, 0, 0))),
        pl.BlockSpec(g2.shape, lambda b, t: (0, 0)),
        pl.BlockSpec(w_up.shape, lambda b, t: (0, 0), pipeline_mode=once),
        pl.BlockSpec(conv_w.shape, lambda b, t: (0, 0)),
        pl.BlockSpec(conv_b.shape, lambda b, t: (0, 0)),
        pl.BlockSpec(w_down.shape, lambda b, t: (0, 0), pipeline_mode=once),
    ]
    args = [x, x, x, m, g2, w_up, conv_w, conv_b, w_down]
    if aliased:
        in_specs = [pl.BlockSpec(memory_space=pl.ANY)] + in_specs
        args = [prev_out] + args
    return pl.pallas_call(
        functools.partial(_ffn_kernel, d_ff=d_ff, aliased=aliased),
        grid=(B, n_rows // tm),
        in_specs=in_specs,
        out_specs=pl.BlockSpec((1, tm, D), lambda b, t: (b, t0 + t, 0)),
        out_shape=jax.ShapeDtypeStruct((B, out_rows, D), F32),
        scratch_shapes=[pltpu.VMEM((tm + 2 * FFN_HALO, D), BF16), pltpu.VMEM((tm, d_ff), BF16)],
        input_output_aliases={0: 0} if aliased else {},
        compiler_params=_cparams(("parallel", "arbitrary")),
        name="conv_ffn",
    )(*args)


def _rope_tables_T(rows, rot_dim, n_ctx):
    n_freq = rot_dim // 4
    inv = ROPE_BASE ** (-jnp.arange(n_freq, dtype=F32) / n_freq)
    row = jnp.broadcast_to(jnp.arange(rows, dtype=F32)[:, None], (rows, GRID_W)).reshape(-1)
    col = jnp.broadcast_to(jnp.arange(GRID_W, dtype=F32)[None, :], (rows, GRID_W)).reshape(-1)
    ang = jnp.concatenate([row[:, None] * inv, col[:, None] * inv], axis=-1)
    cosT = jnp.concatenate([jnp.cos(ang).T, jnp.ones((rot_dim // 2, n_ctx), F32)], axis=1)
    sinT = jnp.concatenate([jnp.sin(ang).T, jnp.zeros((rot_dim // 2, n_ctx), F32)], axis=1)
    return cosT, sinT


def _dft_tables(n):
    idx = (jnp.arange(n, dtype=jnp.int32)[:, None] * jnp.arange(n, dtype=jnp.int32)[None, :]) % n
    ang = idx.astype(F32) * (2.0 * math.pi / n)
    return jnp.cos(ang), jnp.sin(ang)


def _dft_tables_split(n, r):
    l = jnp.arange(n, dtype=jnp.int32)[:, None]
    kk = jnp.arange(r, dtype=jnp.int32)[None, :]
    a = ((kk * l) % r).astype(F32) * (2.0 * math.pi / r)
    b = ((kk * l) % n).astype(F32) * (2.0 * math.pi / n)
    ca, sa, cb, sb = jnp.cos(a)[:, :, None], jnp.sin(a)[:, :, None], jnp.cos(b)[:, None, :], jnp.sin(b)[:, None, :]
    return (ca * cb - sa * sb).reshape(n, n), (sa * cb + ca * sb).reshape(n, n)


def _block_diag(w):
    g, a, b = w.shape
    eye = jnp.eye(g, dtype=w.dtype)
    return (eye[:, None, :, None] * w[:, :, None, :]).reshape(g * a, g * b)


def _col(v):
    return v.reshape(-1, 1).astype(F32)


def kernel(x, c, ctx, c_ctx, mod_w, mod_b, norm1_g, norm2_g, mla_w_in, mla_cq_g, mla_ckv_g, mla_w_uq, mla_w_ukv,
           mla_q_g, mla_k_g, fnet_w, even_w_out, win_w_in, win_q_g, win_k_g, win_sink, pool_w, pool_scale, odd_w_out,
           ffn_up, ffn_conv_w, ffn_conv_b, ffn_down):
    B, L, D = x.shape
    C = ctx.shape[1]
    depth = mod_w.shape[0]
    assert L % TOK_TILE == 0 and C == TOK_TILE and B % FNET_BGRP == 0
    n_lat_tiles = L // TOK_TILE
    rows = L // GRID_W

    pad = (-(B + 1)) % 8
    cc = jnp.concatenate([c, c_ctx[None, :], jnp.zeros((pad, D), F32)], axis=0)
    mods = _modulation(cc, mod_w, mod_b).reshape(depth, B + 1 + pad, N_MOD, D)

    cos_a, sin_a = _rope_tables_T(rows, A_ROPE, C)
    cos_w, sin_w = _rope_tables_T(rows, C_HDIM, C)
    r_lat = math.isqrt(L)
    c_lat, s_lat = (t.astype(BF16) for t in (_dft_tables_split(L, r_lat) if r_lat * r_lat == L else _dft_tables(L)))
    c_ctx_t, s_ctx_t = (t.astype(BF16) for t in _dft_tables(C))
    cc64, ss64 = _dft_tables(B_GDIM)
    eye_g = jnp.eye(B_GROUPS, dtype=F32)
    cs_bd = jnp.concatenate([jnp.kron(eye_g, cc64), -jnp.kron(eye_g, ss64)], axis=0).astype(BF16)

    nh = C_HEADS // (C_KV_HEADS // 2)
    head_order = [p * nh + e * (nh // 2) + j for p in range(C_KV_HEADS // 2) for j in range(nh // 2) for e in range(2)]
    att_perm = jnp.asarray([h * C_HDIM + d for h in head_order for d in range(C_HDIM)], jnp.int32)

    h = jnp.concatenate([x, ctx], axis=1)
    for i in range(depth):
        j = i // 2
        m = mods[i]
        g1 = norm1_g[i][None, :]
        if i % 2 == 0:
            q, kT, vT, z = _even_in(
                h, m, g1, mla_w_in[j].T.astype(BF16), _col(mla_cq_g[j]), mla_w_uq[j].T.astype(BF16),
                _col(mla_q_g[j]), _col(mla_ckv_g[j]), mla_w_ukv[j].T.astype(BF16), _col(mla_k_g[j]),
                cos_a, sin_a, n_lat_tiles)
            att = _mla_attn(q, kT, vT, 0, L, MLA_Q_TILE, 0, L + C)
            att = _mla_attn(q, kT, vT, L, C, C, L, C, prev_out=att)
            wf_bd = _block_diag(fnet_w[j]).astype(BF16)
            br = _fnet(z, c_lat, s_lat, cs_bd, wf_bd, 0, L + C)
            br = _fnet(z, c_ctx_t, s_ctx_t, cs_bd, wf_bd, L, L + C, prev_out=br)
            w_out = even_w_out[j].astype(BF16)
            wa, wb = w_out[:A_HEADS * A_V], w_out[A_HEADS * A_V:]
        else:
            q, k2, v2, pz = _odd_in(h, m, g1, win_w_in[j].T.astype(BF16), _col(win_q_g[j]), _col(win_k_g[j]),
                                    cos_w, sin_w, n_lat_tiles)
            att = _win_attn(win_sink[j].reshape(-1), q, k2, v2, L)
            br = _pool(pz, _block_diag(pool_w[j]).astype(BF16), pool_scale[j][None, :], L)
            w_out = odd_w_out[j].astype(BF16)
            wa, wb = w_out[:C_QW][att_perm], w_out[C_QW:]
        h = _out_proj(h, m, att, br, wa, wb, n_lat_tiles)
        ffn_args = (m, norm2_g[i][None, :], ffn_up[i].astype(BF16), ffn_conv_w[i], ffn_conv_b[i][None, :],
                    ffn_down[i].astype(BF16))
        h_in = h
        if i < depth - 1:
            h = _ffn(h_in, *ffn_args, 0, L, FFN_LAT_TILE, False, L + C)
            h = _ffn(h_in, *ffn_args, L, C, C, True, L + C, prev_out=h)
        else:
            h = _ffn(h_in, *ffn_args, 0, L, FFN_LAT_TILE, False, L)
    return h
```

```python
import functools
import math

import jax
import jax.numpy as jnp
from jax import lax
from jax.experimental import pallas as pl
from jax.experimental.pallas import tpu as pltpu

F32 = jnp.float32
BF16 = jnp.bfloat16

GRID_W = 64
N_MOD = 6
EPS = 1e-6
ROPE_BASE = 10000.0
NEG_INF = -1e30
A_HEADS = 12
A_NOPE = 64
A_ROPE = 32
A_QK = A_NOPE + A_ROPE
A_V = 64
A_Q_LORA = 256
A_KV_LORA = 128
A_IN = A_Q_LORA + A_KV_LORA + A_ROPE
B_GROUPS = 4
B_GDIM = 64
B_WIDTH = B_GROUPS * B_GDIM
C_HEADS = 12
C_KV_HEADS = 4
C_GROUP = C_HEADS // C_KV_HEADS
C_HDIM = 64
C_WINDOW = 128
C_QW = C_HEADS * C_HDIM
C_KW = C_KV_HEADS * C_HDIM
D_GROUPS = 4
D_GDIM = 64
D_WIDTH = D_GROUPS * D_GDIM
D_WINDOWS = (2, 4, 8, 16)

LANES = 128
BF16_ROWS = 16
HEAD_PAD = LANES
TOK_TILE = 256
FFN_HALO = 8
FFN_CHUNK = 256
FFN_LAT_TILE = 512
FNET_BGRP = 4
IN_TILE = 512
MLA_Q_TILE = 512
MLA_KEY_CHUNK = 256
MLA_PV_LAG = 1
LOG2_E = math.log2(math.e)
WIN_KEY_CHUNK = 256
VMEM_LIMIT = 56 * 1024 * 1024

NT_DIMS = (((1,), (1,)), ((), ()))


def _cparams(sem):
    return pltpu.CompilerParams(dimension_semantics=sem, vmem_limit_bytes=VMEM_LIMIT)


def _mod_norm(x, g, shift, scale):
    ms = jnp.mean(x * x, axis=-1, keepdims=True)
    y = x * lax.rsqrt(ms + EPS) * g
    return y * (1.0 + scale) + shift


def _rms_rows(xT, g_col):
    ms = jnp.mean(xT * xT, axis=0, keepdims=True)
    return xT * lax.rsqrt(ms + EPS) * g_col


def _rope_rows(x1, x2, cos, sin):
    return x1 * cos - x2 * sin, x1 * sin + x2 * cos


def _mod_kernel(cc_ref, w_ref, b_ref, o_ref):
    cc = cc_ref[...]
    a = (cc * jax.nn.sigmoid(cc)).astype(BF16)
    w = w_ref[0].astype(BF16)
    o_ref[0] = jnp.dot(a, w, preferred_element_type=F32) + b_ref[0]


def _modulation(cc, mod_w, mod_b):
    depth, d, n = mod_w.shape
    rows = cc.shape[0]
    tn = 1024
    return pl.pallas_call(
        _mod_kernel,
        grid=(depth, n // tn),
        in_specs=[
            pl.BlockSpec((rows, d), lambda i, j: (0, 0)),
            pl.BlockSpec((1, d, tn), lambda i, j: (i, 0, j)),
            pl.BlockSpec((1, 1, tn), lambda i, j: (i, 0, j)),
        ],
        out_specs=pl.BlockSpec((1, rows, tn), lambda i, j: (i, 0, j)),
        out_shape=jax.ShapeDtypeStruct((depth, rows, n), F32),
        compiler_params=_cparams(("arbitrary", "arbitrary")),
        name="modulation",
    )(cc, mod_w, mod_b.reshape(depth, 1, n))


def _even_in_kernel(x_ref, m_ref, g1_ref, w_in_ref, cqg_ref, wuq_ref, qg_ref, ckvg_ref, wukv_ref, kg_ref,
                    cos_ref, sin_ref, q_ref, k_ref, v_ref, z_ref):
    tm = x_ref.shape[1]
    m = m_ref[0]
    u = _mod_norm(x_ref[0], g1_ref[...], m[0:1], m[1:2]).astype(BF16)
    hT = lax.dot_general(w_in_ref[...], u, NT_DIMS, preferred_element_type=F32)
    cos = cos_ref[...]
    sin = sin_ref[...]
    scale = A_QK ** -0.5 * LOG2_E
    half = A_ROPE // 2
    zpad = jnp.zeros((HEAD_PAD - A_QK, tm), F32)
    ones_rows = jnp.ones((BF16_ROWS, tm), F32)

    z_ref[0] = hT[A_IN:A_IN + B_WIDTH].T.astype(BF16)

    cqn = _rms_rows(hT[0:A_Q_LORA], cqg_ref[...]).astype(BF16)
    qT = jnp.dot(wuq_ref[...], cqn, preferred_element_type=F32)
    qg = qg_ref[...] * scale
    for h in range(A_HEADS):
        qn = _rms_rows(qT[h * A_QK:(h + 1) * A_QK], qg)
        o1, o2 = _rope_rows(qn[A_NOPE:A_NOPE + half], qn[A_NOPE + half:A_QK], cos, sin)
        full = jnp.concatenate([qn[0:A_NOPE], o1, o2, zpad], axis=0)
        q_ref[0, h] = full.astype(BF16)

    ckvn = _rms_rows(hT[A_Q_LORA:A_Q_LORA + A_KV_LORA], ckvg_ref[...]).astype(BF16)
    kvT = jnp.dot(wukv_ref[...], ckvn, preferred_element_type=F32)
    kr = hT[A_Q_LORA + A_KV_LORA:A_IN]
    kr_ss = jnp.sum(kr * kr, axis=0, keepdims=True)
    kg = kg_ref[...]
    hw = A_NOPE + A_V
    for h in range(A_HEADS):
        kn = kvT[h * hw:h * hw + A_NOPE]
        r = lax.rsqrt((jnp.sum(kn * kn, axis=0, keepdims=True) + kr_ss) * (1.0 / A_QK) + EPS)
        knn = kn * r * kg[0:A_NOPE]
        krn = kr * r * kg[A_NOPE:A_QK]
        o1, o2 = _rope_rows(krn[0:half], krn[half:A_ROPE], cos, sin)
        k_ref[0, h] = jnp.concatenate([knn, o1, o2, zpad], axis=0).T.astype(BF16)
        vh = kvT[h * hw + A_NOPE:(h + 1) * hw]
        v_ref[0, h] = jnp.concatenate([vh, ones_rows], axis=0).astype(BF16)


def _even_in(x, m, g1, w_inT, cq_g, w_uqT, q_g, ckv_g, w_ukvT, k_g, cosT, sinT, n_lat_tiles):
    B, Lt, D = x.shape
    tm = IN_TILE
    nt = pl.cdiv(Lt, tm)
    n_lat_tiles = (n_lat_tiles * TOK_TILE) // tm
    full2 = lambda a: pl.BlockSpec(a.shape, lambda b, t: (0, 0))
    return pl.pallas_call(
        _even_in_kernel,
        grid=(B, nt),
        in_specs=[
            pl.BlockSpec((1, tm, D), lambda b, t: (b, t, 0)),
            pl.BlockSpec((1, N_MOD, D), lambda b, t: (jnp.where(t < n_lat_tiles, b, B), 0, 0)),
            full2(g1), full2(w_inT), full2(cq_g), full2(w_uqT), full2(q_g), full2(ckv_g), full2(w_ukvT), full2(k_g),
            pl.BlockSpec((A_ROPE // 2, tm), lambda b, t: (0, t)),
            pl.BlockSpec((A_ROPE // 2, tm), lambda b, t: (0, t)),
        ],
        out_specs=[
            pl.BlockSpec((1, A_HEADS, HEAD_PAD, tm), lambda b, t: (b, 0, 0, t)),
            pl.BlockSpec((1, A_HEADS, tm, HEAD_PAD), lambda b, t: (b, 0, t, 0)),
            pl.BlockSpec((1, A_HEADS, A_V + BF16_ROWS, tm), lambda b, t: (b, 0, 0, t)),
            pl.BlockSpec((1, tm, B_WIDTH), lambda b, t: (b, t, 0)),
        ],
        out_shape=[
            jax.ShapeDtypeStruct((B, A_HEADS, HEAD_PAD, Lt), BF16),
            jax.ShapeDtypeStruct((B, A_HEADS, Lt, HEAD_PAD), BF16),
            jax.ShapeDtypeStruct((B, A_HEADS, A_V + BF16_ROWS, Lt), BF16),
            jax.ShapeDtypeStruct((B, Lt, B_WIDTH), BF16),
        ],
        compiler_params=_cparams(("parallel", "arbitrary")),
        name="even_in",
    )(x, m, g1, w_inT, cq_g, w_uqT, q_g, ckv_g, w_ukvT, k_g, cosT, sinT)


def _mla_attn_kernel(*refs, aliased):
    q_ref, k_ref, v_ref, o_ref = refs[1:] if aliased else refs
    nk = k_ref.shape[2]
    qs = [q_ref[0, j] for j in range(2)]
    m = [None, None]
    acc = [None, None]
    pend = [[], []]
    chunks = [(k0, min(k0 + MLA_KEY_CHUNK, nk)) for k0 in range(0, nk, MLA_KEY_CHUNK)]

    def scores(j, k0, k1):
        s = jnp.dot(k_ref[0, j, k0:k1, :], qs[j], preferred_element_type=F32)
        mx = jnp.max(s, axis=0, keepdims=True)
        m_new = mx if m[j] is None else jnp.maximum(m[j], mx)
        alpha = None if m[j] is None else jnp.exp2(m[j] - m_new)
        pend[j].append((jnp.exp2(s - m_new).astype(BF16), alpha, k0, k1))
        m[j] = m_new

    def weighted_values(j):
        p, alpha, k0, k1 = pend[j].pop(0)
        pv = jnp.dot(v_ref[0, j, :, k0:k1], p, preferred_element_type=F32)
        acc[j] = pv if alpha is None else alpha * acc[j] + pv

    for i in range(len(chunks) + MLA_PV_LAG):
        for j in range(2):
            if i < len(chunks):
                scores(j, *chunks[i])
            if i >= MLA_PV_LAG:
                weighted_values(j)
    outs = [(a / a[A_V:A_V + 1])[0:A_V] for a in acc]
    o_ref[0] = jnp.concatenate(outs, axis=0).T.astype(o_ref.dtype)


def _mla_attn(q, kT, vT, q_row0, n_q, tq, k_col0, n_k, prev_out=None):
    B, H, _, Lt = q.shape
    assert q_row0 % tq == 0 and n_q % tq == 0 and k_col0 % n_k == 0
    qb, kb = q_row0 // tq, k_col0 // n_k
    aliased = prev_out is not None
    in_specs = [
        pl.BlockSpec((1, 2, HEAD_PAD, tq), lambda b, h, t: (b, h, 0, qb + t)),
        pl.BlockSpec((1, 2, n_k, HEAD_PAD), lambda b, h, t: (b, h, kb, 0)),
        pl.BlockSpec((1, 2, A_V + BF16_ROWS, n_k), lambda b, h, t: (b, h, 0, kb)),
    ]
    args = [q, kT, vT]
    if aliased:
        in_specs = [pl.BlockSpec(memory_space=pl.ANY)] + in_specs
        args = [prev_out] + args
    return pl.pallas_call(
        functools.partial(_mla_attn_kernel, aliased=aliased),
        grid=(B, H // 2, n_q // tq),
        in_specs=in_specs,
        out_specs=pl.BlockSpec((1, tq, 2 * A_V), lambda b, h, t: (b, qb + t, h)),
        out_shape=jax.ShapeDtypeStruct((B, Lt, H * A_V), BF16),
        input_output_aliases={0: 0} if aliased else {},
        compiler_params=_cparams(("parallel", "arbitrary", "arbitrary")),
        name="mla_attn",
    )(*args)


def _fnet_kernel(*refs, norm, aliased):
    if aliased:
        refs = refs[1:]
    c_ref, s_ref, z_ref, cs_ref, wf_ref, o_ref, p_acc, q_acc = refs
    k = pl.program_id(2)

    @pl.when(k == 0)
    def _():
        p_acc[...] = jnp.zeros_like(p_acc)
        q_acc[...] = jnp.zeros_like(q_acc)

    c = c_ref[...]
    s = s_ref[...]
    for j in range(z_ref.shape[0]):
        z = z_ref[j]
        p_acc[j] += jnp.dot(c, z, preferred_element_type=F32)
        q_acc[j] += jnp.dot(s, z, preferred_element_type=F32)

    @pl.when(k == pl.num_programs(2) - 1)
    def _():
        for j in range(z_ref.shape[0]):
            pq = jnp.concatenate([p_acc[j], q_acc[j]], axis=1).astype(BF16)
            y = jnp.dot(pq, cs_ref[...], preferred_element_type=F32) * norm
            o_ref[j] = jnp.dot(y.astype(BF16), wf_ref[...], preferred_element_type=F32).astype(o_ref.dtype)


def _fnet(z, c_tab, s_tab, cs_bd, wf_bd, row0, out_rows, prev_out=None):
    B = z.shape[0]
    n = c_tab.shape[0]
    tm = min(n, 1024)
    tk = min(n, 1024)
    g = FNET_BGRP
    norm = 1.0 / math.sqrt(n * B_GDIM)
    aliased = prev_out is not None
    in_specs = [
        pl.BlockSpec((tm, tk), lambda b, i, k: (i, k)),
        pl.BlockSpec((tm, tk), lambda b, i, k: (i, k)),
        pl.BlockSpec((g, tk, B_WIDTH), lambda b, i, k: (b, row0 // tk + k, 0)),
        pl.BlockSpec(cs_bd.shape, lambda b, i, k: (0, 0)),
        pl.BlockSpec(wf_bd.shape, lambda b, i, k: (0, 0)),
    ]
    args = [c_tab, s_tab, z, cs_bd, wf_bd]
    if aliased:
        in_specs = [pl.BlockSpec(memory_space=pl.ANY)] + in_specs
        args = [prev_out] + args
    return pl.pallas_call(
        functools.partial(_fnet_kernel, norm=norm, aliased=aliased),
        grid=(B // g, n // tm, n // tk),
        in_specs=in_specs,
        out_specs=pl.BlockSpec((g, tm, B_WIDTH), lambda b, i, k: (b, row0 // tm + i, 0)),
        out_shape=jax.ShapeDtypeStruct((B, out_rows, B_WIDTH), BF16),
        scratch_shapes=[pltpu.VMEM((g, tm, B_WIDTH), F32), pltpu.VMEM((g, tm, B_WIDTH), F32)],
        input_output_aliases={0: 0} if aliased else {},
        compiler_params=_cparams(("parallel", "arbitrary", "arbitrary")),
        name="fnet",
    )(*args)


def _odd_in_kernel(x_ref, m_ref, g1_ref, w_in_ref, qg_ref, kg_ref, cos_ref, sin_ref, q_ref, k_ref, v_ref, pz_ref):
    tm = x_ref.shape[1]
    m = m_ref[0]
    u = _mod_norm(x_ref[0], g1_ref[...], m[0:1], m[1:2]).astype(BF16)
    hT = lax.dot_general(w_in_ref[...], u, NT_DIMS, preferred_element_type=F32)
    cos = cos_ref[...]
    sin = sin_ref[...]
    half = C_HDIM // 2
    zeros = jnp.zeros((C_HDIM, tm), F32)

    def norm_rope(xT, g_col):
        xn = _rms_rows(xT, g_col)
        o1, o2 = _rope_rows(xn[0:half], xn[half:C_HDIM], cos, sin)
        return jnp.concatenate([o1, o2], axis=0)

    qg = qg_ref[...] * (C_HDIM ** -0.5 * LOG2_E)
    for h in range(C_HEADS):
        qh = norm_rope(hT[h * C_HDIM:(h + 1) * C_HDIM], qg)
        parts = [qh, zeros] if (h // C_GROUP) % 2 == 0 else [zeros, qh]
        q_ref[0, h] = jnp.concatenate(parts, axis=0).astype(BF16)

    kg = kg_ref[...]
    for p in range(C_KV_HEADS // 2):
        ks = [norm_rope(hT[C_QW + (2 * p + e) * C_HDIM:C_QW + (2 * p + e + 1) * C_HDIM], kg) for e in range(2)]
        k_ref[0, p] = jnp.concatenate(ks, axis=0).T.astype(BF16)
        v0 = C_QW + C_KW + 2 * p * C_HDIM
        v_ref[0, p] = hT[v0:v0 + 2 * C_HDIM].T.astype(BF16)

    pz_ref[0] = hT[C_QW + 2 * C_KW:C_QW + 2 * C_KW + D_WIDTH].T


def _odd_in(x, m, g1, w_inT, q_g, k_g, cosT, sinT, n_lat_tiles):
    B, Lt, D = x.shape
    tm = TOK_TILE
    nt = Lt // tm
    full2 = lambda a: pl.BlockSpec(a.shape, lambda b, t: (0, 0))
    np_ = C_KV_HEADS // 2
    return pl.pallas_call(
        _odd_in_kernel,
        grid=(B, nt),
        in_specs=[
            pl.BlockSpec((1, tm, D), lambda b, t: (b, t, 0)),
            pl.BlockSpec((1, N_MOD, D), lambda b, t: (jnp.where(t < n_lat_tiles, b, B), 0, 0)),
            full2(g1), full2(w_inT), full2(q_g), full2(k_g),
            pl.BlockSpec((C_HDIM // 2, tm), lambda b, t: (0, t)),
            pl.BlockSpec((C_HDIM // 2, tm), lambda b, t: (0, t)),
        ],
        out_specs=[
            pl.BlockSpec((1, C_HEADS, HEAD_PAD, tm), lambda b, t: (b, 0, 0, t)),
            pl.BlockSpec((1, np_, tm, HEAD_PAD), lambda b, t: (b, 0, t, 0)),
            pl.BlockSpec((1, np_, tm, HEAD_PAD), lambda b, t: (b, 0, t, 0)),
            pl.BlockSpec((1, tm, D_WIDTH), lambda b, t: (b, t, 0)),
        ],
        out_shape=[
            jax.ShapeDtypeStruct((B, C_HEADS, HEAD_PAD, Lt), BF16),
            jax.ShapeDtypeStruct((B, np_, Lt, HEAD_PAD), BF16),
            jax.ShapeDtypeStruct((B, np_, Lt, HEAD_PAD), BF16),
            jax.ShapeDtypeStruct((B, Lt, D_WIDTH), F32),
        ],
        compiler_params=_cparams(("parallel", "arbitrary")),
        name="odd_in",
    )(x, m, g1, w_inT, q_g, k_g, cosT, sinT)


def _win_attn_kernel(sink_ref, q_ref, k_ref, v_ref, o_ref, *, n_lat_tiles, n_lat):
    pair = pl.program_id(1)
    t = pl.program_id(2)
    nh = q_ref.shape[1]
    tq = q_ref.shape[3]
    half = nh // 2
    ck = WIN_KEY_CHUNK
    col = lax.broadcasted_iota(jnp.int32, (1, 2 * tq), 1)
    row = lax.broadcasted_iota(jnp.int32, (HEAD_PAD, tq), 0)

    def run(chunks):
        qg = [jnp.concatenate([q_ref[0, j], q_ref[0, half + j]], axis=1) for j in range(half)]
        sk = [jnp.where(col < tq, sink_ref[pair * nh + j], sink_ref[pair * nh + half + j]) * LOG2_E
              for j in range(half)]
        m = list(sk)
        acc = [None] * half
        pend = [[] for _ in range(half)]

        def scores(j, kb, valid):
            s = jnp.dot(kb, qg[j], preferred_element_type=F32)
            if valid is not None:
                s = jnp.where(valid, s, NEG_INF)
            m_new = jnp.maximum(m[j], jnp.max(s, axis=0, keepdims=True))
            pend[j].append((jnp.exp2(s - m_new).astype(BF16), jnp.exp2(m[j] - m_new)))
            m[j] = m_new

        def weighted_values(j, vbT):
            p, alpha = pend[j].pop(0)
            pv = jnp.dot(vbT, p, preferred_element_type=F32)
            acc[j] = pv if acc[j] is None else alpha * acc[j] + pv

        for i in range(len(chunks) + 1):
            for j in range(half):
                if i < len(chunks):
                    scores(j, chunks[i][0], chunks[i][2])
                if i >= 1:
                    weighted_values(j, chunks[i - 1][1])
        for j in range(half):
            l = acc[j][HEAD_PAD:HEAD_PAD + 1] + jnp.exp2(sk[j] - m[j])
            o = acc[j][0:HEAD_PAD] / l
            blk = jnp.where(row < C_HDIM, o[:, 0:tq], o[:, tq:2 * tq])
            o_ref[0, :, j * HEAD_PAD:(j + 1) * HEAD_PAD] = blk.T.astype(o_ref.dtype)

    def kv_chunk(start):
        kb = k_ref[0, 0, pl.ds(start, ck), :]
        vT = v_ref[0, 0, pl.ds(start, ck), :].astype(F32).T
        return kb, jnp.concatenate([vT, jnp.ones((BF16_ROWS, ck), F32)], axis=0).astype(BF16)

    @pl.when(t < n_lat_tiles)
    def _():
        start = pl.multiple_of(jnp.maximum(t * tq - C_WINDOW, 0), C_WINDOW)
        qpos = t * tq + (lax.broadcasted_iota(jnp.int32, (ck, 2 * tq), 1) & (tq - 1))
        chunks = []
        for c in range((tq + 2 * C_WINDOW) // ck):
            kpos = start + c * ck + lax.broadcasted_iota(jnp.int32, (ck, 2 * tq), 0)
            valid = (jnp.abs(kpos - qpos) <= C_WINDOW) & (kpos < n_lat)
            chunks.append(kv_chunk(start + c * ck) + (valid,))
        chunks.append(kv_chunk(n_lat) + (None,))
        run(chunks)

    @pl.when(t >= n_lat_tiles)
    def _():
        run([kv_chunk(n_lat) + (None,)])


def _win_attn(sink, q, k2, v2, n_lat):
    B, H, _, Lt = q.shape
    tq = TOK_TILE
    npair = k2.shape[1]
    nh = H // npair
    return pl.pallas_call(
        functools.partial(_win_attn_kernel, n_lat_tiles=n_lat // tq, n_lat=n_lat),
        grid=(B, npair, Lt // tq),
        in_specs=[
            pl.BlockSpec(memory_space=pltpu.SMEM),
            pl.BlockSpec((1, nh, HEAD_PAD, tq), lambda b, p, t: (b, p, 0, t)),
            pl.BlockSpec((1, 1, Lt, HEAD_PAD), lambda b, p, t: (b, p, 0, 0)),
            pl.BlockSpec((1, 1, Lt, HEAD_PAD), lambda b, p, t: (b, p, 0, 0)),
        ],
        out_specs=pl.BlockSpec((1, tq, nh * C_HDIM), lambda b, p, t: (b, t, p)),
        out_shape=jax.ShapeDtypeStruct((B, Lt, H * C_HDIM), BF16),
        compiler_params=_cparams(("parallel", "arbitrary", "arbitrary")),
        name="win_attn",
    )(sink, q, k2, v2)


def _pool_lane_tile(z, wp, ps, g0):
    n = z.shape[0]
    row = lax.broadcasted_iota(jnp.int32, z.shape, 0)
    low = lax.broadcasted_iota(jnp.int32, z.shape, 1) < D_GDIM

    def shift_down(a, s):
        return jnp.where(row >= s, pltpu.roll(a, s, 0), 0.0)

    def shift_up(a, s):
        return jnp.where(row < n - s, pltpu.roll(a, n - s, 0), 0.0)

    h_lo, h_hi = D_WINDOWS[g0] // 2, D_WINDOWS[g0 + 1] // 2
    back, fwd, h = z, z, 1
    levels = {1: (z, z)}
    while h < h_hi:
        back = back + shift_down(back, h)
        fwd = fwd + shift_up(fwd, h)
        h *= 2
        levels[h] = (back, fwd)
    back_sel = jnp.where(low, levels[h_lo][0], levels[h_hi][0])
    fwd_sel = jnp.where(low, levels[h_lo][1], levels[h_hi][1])
    win_sum = shift_down(back_sel, 1) + fwd_sel
    half = jnp.where(low, h_lo, h_hi)
    cnt = jnp.minimum(row + half, n) - jnp.maximum(row - half, 0)
    pooled = (win_sum / cnt.astype(F32) - z).astype(BF16)
    return jnp.dot(pooled, wp, preferred_element_type=F32) * ps


def _pool_kernel(pz_ref, wp_ref, ps_ref, o_ref, *, n_lat):
    lt = pz_ref.shape[1]
    for lo, hi in ((0, n_lat), (n_lat, lt)):
        for c in range(D_GROUPS // 2):
            c0, c1 = c * LANES, (c + 1) * LANES
            y = _pool_lane_tile(pz_ref[0, lo:hi, c0:c1], wp_ref[c0:c1, c0:c1], ps_ref[:, c0:c1], 2 * c)
            o_ref[0, lo:hi, c0:c1] = y.astype(o_ref.dtype)


def _pool(pz, wp_bd, p_scale, n_lat):
    B, Lt, W = pz.shape
    return pl.pallas_call(
        functools.partial(_pool_kernel, n_lat=n_lat),
        grid=(B,),
        in_specs=[
            pl.BlockSpec((1, Lt, W), lambda b: (b, 0, 0)),
            pl.BlockSpec(wp_bd.shape, lambda b: (0, 0)),
            pl.BlockSpec(p_scale.shape, lambda b: (0, 0)),
        ],
        out_specs=pl.BlockSpec((1, Lt, W), lambda b: (b, 0, 0)),
        out_shape=jax.ShapeDtypeStruct((B, Lt, W), BF16),
        compiler_params=_cparams(("parallel",)),
        name="pool",
    )(pz, wp_bd, p_scale)


def _out_kernel(x_ref, m_ref, a_ref, b_ref, wa_ref, wb_ref, o_ref):
    y = (jnp.dot(a_ref[0], wa_ref[...], preferred_element_type=F32)
         + jnp.dot(b_ref[0], wb_ref[...], preferred_element_type=F32))
    o_ref[0] = x_ref[0] + m_ref[0][2:3] * y


def _out_proj(x, m, att, br, wa, wb, n_lat_tiles):
    B, Lt, D = x.shape
    tm = TOK_TILE
    return pl.pallas_call(
        _out_kernel,
        grid=(B, Lt // tm),
        in_specs=[
            pl.BlockSpec((1, tm, D), lambda b, t: (b, t, 0)),
            pl.BlockSpec((1, N_MOD, D), lambda b, t: (jnp.where(t < n_lat_tiles, b, B), 0, 0)),
            pl.BlockSpec((1, tm, att.shape[2]), lambda b, t: (b, t, 0)),
            pl.BlockSpec((1, tm, br.shape[2]), lambda b, t: (b, t, 0)),
            pl.BlockSpec(wa.shape, lambda b, t: (0, 0)),
            pl.BlockSpec(wb.shape, lambda b, t: (0, 0)),
        ],
        out_specs=pl.BlockSpec((1, tm, D), lambda b, t: (b, t, 0)),
        out_shape=jax.ShapeDtypeStruct((B, Lt, D), F32),
        compiler_params=_cparams(("parallel", "arbitrary")),
        name="out_proj",
    )(x, m, att, br, wa, wb)


def _ffn_kernel(*refs, d_ff, aliased):
    if aliased:
        refs = refs[1:]
    x_ref, xp_ref, xn_ref, m_ref, g2_ref, wup_ref, cw_ref, cb_ref, wdn_ref, o_ref, u_scr, act_scr = refs
    t = pl.program_id(1)
    tm = x_ref.shape[1]
    rows = tm + 2 * FFN_HALO
    m = m_ref[0]
    g2 = g2_ref[...]

    def mod_norm_rows(xv):
        return _mod_norm(xv, g2, m[3:4], m[4:5])

    u_scr[0:tm] = mod_norm_rows(x_ref[0]).astype(BF16)
    u_next = jnp.where(t == pl.num_programs(1) - 1, 0.0, mod_norm_rows(xn_ref[0]))
    u_prev = jnp.where(t == 0, 0.0, mod_norm_rows(xp_ref[0]))
    u_scr[tm:rows] = jnp.concatenate([u_next, u_prev], axis=0).astype(BF16)

    def conv(h, col0, width):
        cw = cw_ref[:, col0:col0 + width]
        out = (pltpu.roll(h, 1, 0) * cw[0:1] + h * cw[1:2] + pltpu.roll(h, rows - 1, 0) * cw[2:3]
               + cb_ref[:, col0:col0 + width])
        return out[0:tm]

    u = u_scr[...]
    for c in range(d_ff // FFN_CHUNK):
        c0 = c * FFN_CHUNK
        gate = conv(jnp.dot(u, wup_ref[:, c0:c0 + FFN_CHUNK], preferred_element_type=F32), c0, FFN_CHUNK)
        val = conv(jnp.dot(u, wup_ref[:, d_ff + c0:d_ff + c0 + FFN_CHUNK], preferred_element_type=F32),
                   d_ff + c0, FFN_CHUNK)
        act_scr[:, c0:c0 + FFN_CHUNK] = (gate * jax.nn.sigmoid(gate) * val).astype(BF16)

    y = jnp.dot(act_scr[...], wdn_ref[...], preferred_element_type=F32)
    o_ref[0] = x_ref[0] + m[5:6] * y


def _ffn(x, m, g2, w_up, conv_w, conv_b, w_down, row0, n_rows, tm, ctx_mod, out_rows, prev_out=None):
    B, Lt, D = x.shape
    d_ff = w_down.shape[0]
    assert row0 % tm == 0 and n_rows % tm == 0 and tm % BF16_ROWS == 0 and 2 * FFN_HALO == BF16_ROWS
    t0 = row0 // tm
    hb = tm // FFN_HALO
    h0 = row0 // FFN_HALO
    nhb = Lt // FFN_HALO
    once = pl.Buffered(1)
    aliased = prev_out is not None
    in_specs = [
        pl.BlockSpec((1, tm, D), lambda b, t: (b, t0 + t, 0)),
        pl.BlockSpec((1, FFN_HALO, D), lambda b, t: (b, jnp.maximum(h0 + t * hb - 1, 0), 0)),
        pl.BlockSpec((1, FFN_HALO, D), lambda b, t: (b, jnp.minimum(h0 + (t + 1) * hb, nhb - 1), 0)),
        pl.BlockSpec((1, N_MOD, D), (lambda b, t: (B, 0, 0)) if ctx_mod else (lambda b, t: (b, 0, 0))),
        pl.BlockSpec(g2.shape, lambda b, t: (0, 0)),
        pl.BlockSpec(w_up.shape, lambda b, t: (0, 0), pipeline_mode=once),
        pl.BlockSpec(conv_w.shape, lambda b, t: (0, 0)),
        pl.BlockSpec(conv_b.shape, lambda b, t: (0, 0)),
        pl.BlockSpec(w_down.shape, lambda b, t: (0, 0), pipeline_mode=once),
    ]
    args = [x, x, x, m, g2, w_up, conv_w, conv_b, w_down]
    if aliased:
        in_specs = [pl.BlockSpec(memory_space=pl.ANY)] + in_specs
        args = [prev_out] + args
    return pl.pallas_call(
        functools.partial(_ffn_kernel, d_ff=d_ff, aliased=aliased),
        grid=(B, n_rows // tm),
        in_specs=in_specs,
        out_specs=pl.BlockSpec((1, tm, D), lambda b, t: (b, t0 + t, 0)),
        out_shape=jax.ShapeDtypeStruct((B, out_rows, D), F32),
        scratch_shapes=[pltpu.VMEM((tm + 2 * FFN_HALO, D), BF16), pltpu.VMEM((tm, d_ff), BF16)],
        input_output_aliases={0: 0} if aliased else {},
        compiler_params=_cparams(("parallel", "arbitrary")),
        name="conv_ffn",
    )(*args)


def _rope_tables_T(rows, rot_dim, n_ctx):
    n_freq = rot_dim // 4
    inv = ROPE_BASE ** (-jnp.arange(n_freq, dtype=F32) / n_freq)
    row = jnp.broadcast_to(jnp.arange(rows, dtype=F32)[:, None], (rows, GRID_W)).reshape(-1)
    col = jnp.broadcast_to(jnp.arange(GRID_W, dtype=F32)[None, :], (rows, GRID_W)).reshape(-1)
    ang = jnp.concatenate([row[:, None] * inv, col[:, None] * inv], axis=-1)
    cosT = jnp.concatenate([jnp.cos(ang).T, jnp.ones((rot_dim // 2, n_ctx), F32)], axis=1)
    sinT = jnp.concatenate([jnp.sin(ang).T, jnp.zeros((rot_dim // 2, n_ctx), F32)], axis=1)
    return cosT, sinT


def _dft_tables(n):
    idx = (jnp.arange(n, dtype=jnp.int32)[:, None] * jnp.arange(n, dtype=jnp.int32)[None, :]) % n
    ang = idx.astype(F32) * (2.0 * math.pi / n)
    return jnp.cos(ang), jnp.sin(ang)


def _dft_tables_split(n, r):
    l = jnp.arange(n, dtype=jnp.int32)[:, None]
    kk = jnp.arange(r, dtype=jnp.int32)[None, :]
    a = ((kk * l) % r).astype(F32) * (2.0 * math.pi / r)
    b = ((kk * l) % n).astype(F32) * (2.0 * math.pi / n)
    ca, sa, cb, sb = jnp.cos(a)[:, :, None], jnp.sin(a)[:, :, None], jnp.cos(b)[:, None, :], jnp.sin(b)[:, None, :]
    return (ca * cb - sa * sb).reshape(n, n), (sa * cb + ca * sb).reshape(n, n)


def _block_diag(w):
    g, a, b = w.shape
    eye = jnp.eye(g, dtype=w.dtype)
    return (eye[:, None, :, None] * w[:, :, None, :]).reshape(g * a, g * b)


def _col(v):
    return v.reshape(-1, 1).astype(F32)


def kernel(x, c, ctx, c_ctx, mod_w, mod_b, norm1_g, norm2_g, mla_w_in, mla_cq_g, mla_ckv_g, mla_w_uq, mla_w_ukv,
           mla_q_g, mla_k_g, fnet_w, even_w_out, win_w_in, win_q_g, win_k_g, win_sink, pool_w, pool_scale, odd_w_out,
           ffn_up, ffn_conv_w, ffn_conv_b, ffn_down):
    B, L, D = x.shape
    C = ctx.shape[1]
    depth = mod_w.shape[0]
    assert L % TOK_TILE == 0 and C == TOK_TILE and B % FNET_BGRP == 0
    n_lat_tiles = L // TOK_TILE
    rows = L // GRID_W

    pad = (-(B + 1)) % 8
    cc = jnp.concatenate([c, c_ctx[None, :], jnp.zeros((pad, D), F32)], axis=0)
    mods = _modulation(cc, mod_w, mod_b).reshape(depth, B + 1 + pad, N_MOD, D)

    cos_a, sin_a = _rope_tables_T(rows, A_ROPE, C)
    cos_w, sin_w = _rope_tables_T(rows, C_HDIM, C)
    r_lat = math.isqrt(L)
    c_lat, s_lat = (t.astype(BF16) for t in (_dft_tables_split(L, r_lat) if r_lat * r_lat == L else _dft_tables(L)))
    c_ctx_t, s_ctx_t = (t.astype(BF16) for t in _dft_tables(C))
    cc64, ss64 = _dft_tables(B_GDIM)
    eye_g = jnp.eye(B_GROUPS, dtype=F32)
    cs_bd = jnp.concatenate([jnp.kron(eye_g, cc64), -jnp.kron(eye_g, ss64)], axis=0).astype(BF16)

    nh = C_HEADS // (C_KV_HEADS // 2)
    head_order = [p * nh + e * (nh // 2) + j for p in range(C_KV_HEADS // 2) for j in range(nh // 2) for e in range(2)]
    att_perm = jnp.asarray([h * C_HDIM + d for h in head_order for d in range(C_HDIM)], jnp.int32)

    h = jnp.concatenate([x, ctx], axis=1)
    for i in range(depth):
        j = i // 2
        m = mods[i]
        g1 = norm1_g[i][None, :]
        if i % 2 == 0:
            q, kT, vT, z = _even_in(
                h, m, g1, mla_w_in[j].T.astype(BF16), _col(mla_cq_g[j]), mla_w_uq[j].T.astype(BF16),
                _col(mla_q_g[j]), _col(mla_ckv_g[j]), mla_w_ukv[j].T.astype(BF16), _col(mla_k_g[j]),
                cos_a, sin_a, n_lat_tiles)
            att = _mla_attn(q, kT, vT, 0, L, MLA_Q_TILE, 0, L + C)
            att = _mla_attn(q, kT, vT, L, C, C, L, C, prev_out=att)
            wf_bd = _block_diag(fnet_w[j]).astype(BF16)
            br = _fnet(z, c_lat, s_lat, cs_bd, wf_bd, 0, L + C)
            br = _fnet(z, c_ctx_t, s_ctx_t, cs_bd, wf_bd, L, L + C, prev_out=br)
            w_out = even_w_out[j].astype(BF16)
            wa, wb = w_out[:A_HEADS * A_V], w_out[A_HEADS * A_V:]
        else:
            q, k2, v2, pz = _odd_in(h, m, g1, win_w_in[j].T.astype(BF16), _col(win_q_g[j]), _col(win_k_g[j]),
                                    cos_w, sin_w, n_lat_tiles)
            att = _win_attn(win_sink[j].reshape(-1), q, k2, v2, L)
            br = _pool(pz, _block_diag(pool_w[j]).astype(BF16), pool_scale[j][None, :], L)
            w_out = odd_w_out[j].astype(BF16)
            wa, wb = w_out[:C_QW][att_perm], w_out[C_QW:]
        h = _out_proj(h, m, att, br, wa, wb, n_lat_tiles)
        ffn_args = (m, norm2_g[i][None, :], ffn_up[i].astype(BF16), ffn_conv_w[i], ffn_conv_b[i][None, :],
                    ffn_down[i].astype(BF16))
        h_in = h
        if i < depth - 1:
            h = _ffn(h_in, *ffn_args, 0, L, FFN_LAT_TILE, False, L + C)
            h = _ffn(h_in, *ffn_args, L, C, C, True, L + C, prev_out=h)
        else:
            h = _ffn(h_in, *ffn_args, 0, L, FFN_LAT_TILE, False, L)
    return h
```

```python
import functools
import math

import jax
import jax.numpy as jnp
from jax import lax
from jax.experimental import pallas as pl
from jax.experimental.pallas import tpu as pltpu

F32 = jnp.float32
BF16 = jnp.bfloat16

GRID_W = 64
N_MOD = 6
EPS = 1e-6
ROPE_BASE = 10000.0
NEG_INF = -1e30
A_HEADS = 12
A_NOPE = 64
A_ROPE = 32
A_QK = A_NOPE + A_ROPE
A_V = 64
A_Q_LORA = 256
A_KV_LORA = 128
A_IN = A_Q_LORA + A_KV_LORA + A_ROPE
B_GROUPS = 4
B_GDIM = 64
B_WIDTH = B_GROUPS * B_GDIM
C_HEADS = 12
C_KV_HEADS = 4
C_GROUP = C_HEADS // C_KV_HEADS
C_HDIM = 64
C_WINDOW = 128
C_QW = C_HEADS * C_HDIM
C_KW = C_KV_HEADS * C_HDIM
D_GROUPS = 4
D_GDIM = 64
D_WIDTH = D_GROUPS * D_GDIM
D_WINDOWS = (2, 4, 8, 16)

LANES = 128
BF16_ROWS = 16
HEAD_PAD = LANES
TOK_TILE = 256
FFN_HALO = 8
FFN_CHUNK = 256
FFN_LAT_TILE = 512
FNET_BGRP = 4
IN_TILE = 512
MLA_Q_TILE = 512
MLA_KEY_CHUNK = 256
MLA_PV_LAG = 1
LOG2_E = math.log2(math.e)
WIN_KEY_CHUNK = 256
VMEM_LIMIT = 56 * 1024 * 1024

NT_DIMS = (((1,), (1,)), ((), ()))


def _cparams(sem):
    return pltpu.CompilerParams(dimension_semantics=sem, vmem_limit_bytes=VMEM_LIMIT)


def _mod_norm(x, g, shift, scale):
    ms = jnp.mean(x * x, axis=-1, keepdims=True)
    y = x * lax.rsqrt(ms + EPS) * g
    return y * (1.0 + scale) + shift


def _rms_rows(xT, g_col):
    ms = jnp.mean(xT * xT, axis=0, keepdims=True)
    return xT * lax.rsqrt(ms + EPS) * g_col


def _rope_rows(x1, x2, cos, sin):
    return x1 * cos - x2 * sin, x1 * sin + x2 * cos


def _mod_kernel(cc_ref, w_ref, b_ref, o_ref):
    cc = cc_ref[...]
    a = (cc * jax.nn.sigmoid(cc)).astype(BF16)
    w = w_ref[0].astype(BF16)
    o_ref[0] = jnp.dot(a, w, preferred_element_type=F32) + b_ref[0]


def _modulation(cc, mod_w, mod_b):
    depth, d, n = mod_w.shape
    rows = cc.shape[0]
    tn = 1024
    return pl.pallas_call(
        _mod_kernel,
        grid=(depth, n // tn),
        in_specs=[
            pl.BlockSpec((rows, d), lambda i, j: (0, 0)),
            pl.BlockSpec((1, d, tn), lambda i, j: (i, 0, j)),
            pl.BlockSpec((1, 1, tn), lambda i, j: (i, 0, j)),
        ],
        out_specs=pl.BlockSpec((1, rows, tn), lambda i, j: (i, 0, j)),
        out_shape=jax.ShapeDtypeStruct((depth, rows, n), F32),
        compiler_params=_cparams(("arbitrary", "arbitrary")),
        name="modulation",
    )(cc, mod_w, mod_b.reshape(depth, 1, n))


def _even_in_kernel(x_ref, m_ref, g1_ref, w_in_ref, cqg_ref, wuq_ref, qg_ref, ckvg_ref, wukv_ref, kg_ref,
                    cos_ref, sin_ref, q_ref, k_ref, v_ref, z_ref):
    tm = x_ref.shape[1]
    m = m_ref[0]
    u = _mod_norm(x_ref[0], g1_ref[...], m[0:1], m[1:2]).astype(BF16)
    hT = lax.dot_general(w_in_ref[...], u, NT_DIMS, preferred_element_type=F32)
    cos = cos_ref[...]
    sin = sin_ref[...]
    scale = A_QK ** -0.5 * LOG2_E
    half = A_ROPE // 2
    zpad = jnp.zeros((HEAD_PAD - A_QK, tm), F32)
    ones_rows = jnp.ones((BF16_ROWS, tm), F32)

    z_ref[0] = hT[A_IN:A_IN + B_WIDTH].T.astype(BF16)

    cqn = _rms_rows(hT[0:A_Q_LORA], cqg_ref[...]).astype(BF16)
    qT = jnp.dot(wuq_ref[...], cqn, preferred_element_type=F32)
    qg = qg_ref[...] * scale
    for h in range(A_HEADS):
        qn = _rms_rows(qT[h * A_QK:(h + 1) * A_QK], qg)
        o1, o2 = _rope_rows(qn[A_NOPE:A_NOPE + half], qn[A_NOPE + half:A_QK], cos, sin)
        full = jnp.concatenate([qn[0:A_NOPE], o1, o2, zpad], axis=0)
        q_ref[0, h] = full.astype(BF16)

    ckvn = _rms_rows(hT[A_Q_LORA:A_Q_LORA + A_KV_LORA], ckvg_ref[...]).astype(BF16)
    kvT = jnp.dot(wukv_ref[...], ckvn, preferred_element_type=F32)
    kr = hT[A_Q_LORA + A_KV_LORA:A_IN]
    kr_ss = jnp.sum(kr * kr, axis=0, keepdims=True)
    kg = kg_ref[...]
    hw = A_NOPE + A_V
    for h in range(A_HEADS):
        kn = kvT[h * hw:h * hw + A_NOPE]
        r = lax.rsqrt((jnp.sum(kn * kn, axis=0, keepdims=True) + kr_ss) * (1.0 / A_QK) + EPS)
        knn = kn * r * kg[0:A_NOPE]
        krn = kr * r * kg[A_NOPE:A_QK]
        o1, o2 = _rope_rows(krn[0:half], krn[half:A_ROPE], cos, sin)
        k_ref[0, h] = jnp.concatenate([knn, o1, o2, zpad], axis=0).T.astype(BF16)
        vh = kvT[h * hw + A_NOPE:(h + 1) * hw]
        v_ref[0, h] = jnp.concatenate([vh, ones_rows], axis=0).astype(BF16)


def _even_in(x, m, g1, w_inT, cq_g, w_uqT, q_g, ckv_g, w_ukvT, k_g, cosT, sinT, n_lat_tiles):
    B, Lt, D = x.shape
    tm = IN_TILE
    nt = pl.cdiv(Lt, tm)
    n_lat_tiles = (n_lat_tiles * TOK_TILE) // tm
    full2 = lambda a: pl.BlockSpec(a.shape, lambda b, t: (0, 0))
    return pl.pallas_call(
        _even_in_kernel,
        grid=(B, nt),
        in_specs=[
            pl.BlockSpec((1, tm, D), lambda b, t: (b, t, 0)),
            pl.BlockSpec((1, N_MOD, D), lambda b, t: (jnp.where(t < n_lat_tiles, b, B), 0, 0)),
            full2(g1), full2(w_inT), full2(cq_g), full2(w_uqT), full2(q_g), full2(ckv_g), full2(w_ukvT), full2(k_g),
            pl.BlockSpec((A_ROPE // 2, tm), lambda b, t: (0, t)),
            pl.BlockSpec((A_ROPE // 2, tm), lambda b, t: (0, t)),
        ],
        out_specs=[
            pl.BlockSpec((1, A_HEADS, HEAD_PAD, tm), lambda b, t: (b, 0, 0, t)),
            pl.BlockSpec((1, A_HEADS, tm, HEAD_PAD), lambda b, t: (b, 0, t, 0)),
            pl.BlockSpec((1, A_HEADS, A_V + BF16_ROWS, tm), lambda b, t: (b, 0, 0, t)),
            pl.BlockSpec((1, tm, B_WIDTH), lambda b, t: (b, t, 0)),
        ],
        out_shape=[
            jax.ShapeDtypeStruct((B, A_HEADS, HEAD_PAD, Lt), BF16),
            jax.ShapeDtypeStruct((B, A_HEADS, Lt, HEAD_PAD), BF16),
            jax.ShapeDtypeStruct((B, A_HEADS, A_V + BF16_ROWS, Lt), BF16),
            jax.ShapeDtypeStruct((B, Lt, B_WIDTH), BF16),
        ],
        compiler_params=_cparams(("parallel", "arbitrary")),
        name="even_in",
    )(x, m, g1, w_inT, cq_g, w_uqT, q_g, ckv_g, w_ukvT, k_g, cosT, sinT)


def _mla_attn_kernel(*refs, aliased):
    q_ref, k_ref, v_ref, o_ref = refs[1:] if aliased else refs
    nk = k_ref.shape[2]
    qs = [q_ref[0, j] for j in range(2)]
    m = [None, None]
    acc = [None, None]
    pend = [[], []]
    chunks = [(k0, min(k0 + MLA_KEY_CHUNK, nk)) for k0 in range(0, nk, MLA_KEY_CHUNK)]

    def scores(j, k0, k1):
        s = jnp.dot(k_ref[0, j, k0:k1, :], qs[j], preferred_element_type=F32)
        mx = jnp.max(s, axis=0, keepdims=True)
        m_new = mx if m[j] is None else jnp.maximum(m[j], mx)
        alpha = None if m[j] is None else jnp.exp2(m[j] - m_new)
        pend[j].append((jnp.exp2(s - m_new).astype(BF16), alpha, k0, k1))
        m[j] = m_new

    def weighted_values(j):
        p, alpha, k0, k1 = pend[j].pop(0)
        pv = jnp.dot(v_ref[0, j, :, k0:k1], p, preferred_element_type=F32)
        acc[j] = pv if alpha is None else alpha * acc[j] + pv

    for i in range(len(chunks) + MLA_PV_LAG):
        for j in range(2):
            if i < len(chunks):
                scores(j, *chunks[i])
            if i >= MLA_PV_LAG:
                weighted_values(j)
    outs = [(a / a[A_V:A_V + 1])[0:A_V] for a in acc]
    o_ref[0] = jnp.concatenate(outs, axis=0).T.astype(o_ref.dtype)


def _mla_attn(q, kT, vT, q_row0, n_q, tq, k_col0, n_k, prev_out=None):
    B, H, _, Lt = q.shape
    assert q_row0 % tq == 0 and n_q % tq == 0 and k_col0 % n_k == 0
    qb, kb = q_row0 // tq, k_col0 // n_k
    aliased = prev_out is not None
    in_specs = [
        pl.BlockSpec((1, 2, HEAD_PAD, tq), lambda b, h, t: (b, h, 0, qb + t)),
        pl.BlockSpec((1, 2, n_k, HEAD_PAD), lambda b, h, t: (b, h, kb, 0)),
        pl.BlockSpec((1, 2, A_V + BF16_ROWS, n_k), lambda b, h, t: (b, h, 0, kb)),
    ]
    args = [q, kT, vT]
    if aliased:
        in_specs = [pl.BlockSpec(memory_space=pl.ANY)] + in_specs
        args = [prev_out] + args
    return pl.pallas_call(
        functools.partial(_mla_attn_kernel, aliased=aliased),
        grid=(B, H // 2, n_q // tq),
        in_specs=in_specs,
        out_specs=pl.BlockSpec((1, tq, 2 * A_V), lambda b, h, t: (b, qb + t, h)),
        out_shape=jax.ShapeDtypeStruct((B, Lt, H * A_V), BF16),
        input_output_aliases={0: 0} if aliased else {},
        compiler_params=_cparams(("parallel", "arbitrary", "arbitrary")),
        name="mla_attn",
    )(*args)


def _fnet_kernel(*refs, norm, aliased):
    if aliased:
        refs = refs[1:]
    c_ref, s_ref, z_ref, cs_ref, wf_ref, o_ref, p_acc, q_acc = refs
    k = pl.program_id(2)

    @pl.when(k == 0)
    def _():
        p_acc[...] = jnp.zeros_like(p_acc)
        q_acc[...] = jnp.zeros_like(q_acc)

    c = c_ref[...]
    s = s_ref[...]
    for j in range(z_ref.shape[0]):
        z = z_ref[j]
        p_acc[j] += jnp.dot(c, z, preferred_element_type=F32)
        q_acc[j] += jnp.dot(s, z, preferred_element_type=F32)

    @pl.when(k == pl.num_programs(2) - 1)
    def _():
        for j in range(z_ref.shape[0]):
            pq = jnp.concatenate([p_acc[j], q_acc[j]], axis=1).astype(BF16)
            y = jnp.dot(pq, cs_ref[...], preferred_element_type=F32) * norm
            o_ref[j] = jnp.dot(y.astype(BF16), wf_ref[...], preferred_element_type=F32).astype(o_ref.dtype)


def _fnet(z, c_tab, s_tab, cs_bd, wf_bd, row0, out_rows, prev_out=None):
    B = z.shape[0]
    n = c_tab.shape[0]
    tm = min(n, 1024)
    tk = min(n, 1024)
    g = FNET_BGRP
    norm = 1.0 / math.sqrt(n * B_GDIM)
    aliased = prev_out is not None
    in_specs = [
        pl.BlockSpec((tm, tk), lambda b, i, k: (i, k)),
        pl.BlockSpec((tm, tk), lambda b, i, k: (i, k)),
        pl.BlockSpec((g, tk, B_WIDTH), lambda b, i, k: (b, row0 // tk + k, 0)),
        pl.BlockSpec(cs_bd.shape, lambda b, i, k: (0, 0)),
        pl.BlockSpec(wf_bd.shape, lambda b, i, k: (0, 0)),
    ]
    args = [c_tab, s_tab, z, cs_bd, wf_bd]
    if aliased:
        in_specs = [pl.BlockSpec(memory_space=pl.ANY)] + in_specs
        args = [prev_out] + args
    return pl.pallas_call(
        functools.partial(_fnet_kernel, norm=norm, aliased=aliased),
        grid=(B // g, n // tm, n // tk),
        in_specs=in_specs,
        out_specs=pl.BlockSpec((g, tm, B_WIDTH), lambda b, i, k: (b, row0 // tm + i, 0)),
        out_shape=jax.ShapeDtypeStruct((B, out_rows, B_WIDTH), BF16),
        scratch_shapes=[pltpu.VMEM((g, tm, B_WIDTH), F32), pltpu.VMEM((g, tm, B_WIDTH), F32)],
        input_output_aliases={0: 0} if aliased else {},
        compiler_params=_cparams(("parallel", "arbitrary", "arbitrary")),
        name="fnet",
    )(*args)


def _odd_in_kernel(x_ref, m_ref, g1_ref, w_in_ref, qg_ref, kg_ref, cos_ref, sin_ref, q_ref, k_ref, v_ref, pz_ref):
    tm = x_ref.shape[1]
    m = m_ref[0]
    u = _mod_norm(x_ref[0], g1_ref[...], m[0:1], m[1:2]).astype(BF16)
    hT = lax.dot_general(w_in_ref[...], u, NT_DIMS, preferred_element_type=F32)
    cos = cos_ref[...]
    sin = sin_ref[...]
    half = C_HDIM // 2
    zeros = jnp.zeros((C_HDIM, tm), F32)

    def norm_rope(xT, g_col):
        xn = _rms_rows(xT, g_col)
        o1, o2 = _rope_rows(xn[0:half], xn[half:C_HDIM], cos, sin)
        return jnp.concatenate([o1, o2], axis=0)

    qg = qg_ref[...] * (C_HDIM ** -0.5 * LOG2_E)
    for h in range(C_HEADS):
        qh = norm_rope(hT[h * C_HDIM:(h + 1) * C_HDIM], qg)
        parts = [qh, zeros] if (h // C_GROUP) % 2 == 0 else [zeros, qh]
        q_ref[0, h] = jnp.concatenate(parts, axis=0).astype(BF16)

    kg = kg_ref[...]
    for p in range(C_KV_HEADS // 2):
        ks = [norm_rope(hT[C_QW + (2 * p + e) * C_HDIM:C_QW + (2 * p + e + 1) * C_HDIM], kg) for e in range(2)]
        k_ref[0, p] = jnp.concatenate(ks, axis=0).T.astype(BF16)
        v0 = C_QW + C_KW + 2 * p * C_HDIM
        v_ref[0, p] = hT[v0:v0 + 2 * C_HDIM].T.astype(BF16)

    pz_ref[0] = hT[C_QW + 2 * C_KW:C_QW + 2 * C_KW + D_WIDTH].T


def _odd_in(x, m, g1, w_inT, q_g, k_g, cosT, sinT, n_lat_tiles):
    B, Lt, D = x.shape
    tm = TOK_TILE
    nt = Lt // tm
    full2 = lambda a: pl.BlockSpec(a.shape, lambda b, t: (0, 0))
    np_ = C_KV_HEADS // 2
    return pl.pallas_call(
        _odd_in_kernel,
        grid=(B, nt),
        in_specs=[
            pl.BlockSpec((1, tm, D), lambda b, t: (b, t, 0)),
            pl.BlockSpec((1, N_MOD, D), lambda b, t: (jnp.where(t < n_lat_tiles, b, B), 0, 0)),
            full2(g1), full2(w_inT), full2(q_g), full2(k_g),
            pl.BlockSpec((C_HDIM // 2, tm), lambda b, t: (0, t)),
            pl.BlockSpec((C_HDIM // 2, tm), lambda b, t: (0, t)),
        ],
        out_specs=[
            pl.BlockSpec((1, C_HEADS, HEAD_PAD, tm), lambda b, t: (b, 0, 0, t)),
            pl.BlockSpec((1, np_, tm, HEAD_PAD), lambda b, t: (b, 0, t, 0)),
            pl.BlockSpec((1, np_, tm, HEAD_PAD), lambda b, t: (b, 0, t, 0)),
            pl.BlockSpec((1, tm, D_WIDTH), lambda b, t: (b, t, 0)),
        ],
        out_shape=[
            jax.ShapeDtypeStruct((B, C_HEADS, HEAD_PAD, Lt), BF16),
            jax.ShapeDtypeStruct((B, np_, Lt, HEAD_PAD), BF16),
            jax.ShapeDtypeStruct((B, np_, Lt, HEAD_PAD), BF16),
            jax.ShapeDtypeStruct((B, Lt, D_WIDTH), F32),
        ],
        compiler_params=_cparams(("parallel", "arbitrary")),
        name="odd_in",
    )(x, m, g1, w_inT, q_g, k_g, cosT, sinT)


def _win_attn_kernel(sink_ref, q_ref, k_ref, v_ref, o_ref, *, n_lat_tiles, n_lat):
    pair = pl.program_id(1)
    t = pl.program_id(2)
    nh = q_ref.shape[1]
    tq = q_ref.shape[3]
    half = nh // 2
    ck = WIN_KEY_CHUNK
    col = lax.broadcasted_iota(jnp.int32, (1, 2 * tq), 1)
    row = lax.broadcasted_iota(jnp.int32, (HEAD_PAD, tq), 0)

    def run(chunks):
        qg = [jnp.concatenate([q_ref[0, j], q_ref[0, half + j]], axis=1) for j in range(half)]
        sk = [jnp.where(col < tq, sink_ref[pair * nh + j], sink_ref[pair * nh + half + j]) * LOG2_E
              for j in range(half)]
        m = list(sk)
        acc = [None] * half
        pend = [[] for _ in range(half)]

        def scores(j, kb, valid):
            s = jnp.dot(kb, qg[j], preferred_element_type=F32)
            if valid is not None:
                s = jnp.where(valid, s, NEG_INF)
            m_new = jnp.maximum(m[j], jnp.max(s, axis=0, keepdims=True))
            pend[j].append((jnp.exp2(s - m_new).astype(BF16), jnp.exp2(m[j] - m_new)))
            m[j] = m_new

        def weighted_values(j, vbT):
            p, alpha = pend[j].pop(0)
            pv = jnp.dot(vbT, p, preferred_element_type=F32)
            acc[j] = pv if acc[j] is None else alpha * acc[j] + pv

        for i in range(len(chunks) + 1):
            for j in range(half):
                if i < len(chunks):
                    scores(j, chunks[i][0], chunks[i][2])
                if i >= 1:
                    weighted_values(j, chunks[i - 1][1])
        for j in range(half):
            l = acc[j][HEAD_PAD:HEAD_PAD + 1] + jnp.exp2(sk[j] - m[j])
            o = acc[j][0:HEAD_PAD] / l
            blk = jnp.where(row < C_HDIM, o[:, 0:tq], o[:, tq:2 * tq])
            o_ref[0, :, j * HEAD_PAD:(j + 1) * HEAD_PAD] = blk.T.astype(o_ref.dtype)

    def kv_chunk(start):
        kb = k_ref[0, 0, pl.ds(start, ck), :]
        vT = v_ref[0, 0, pl.ds(start, ck), :].astype(F32).T
        return kb, jnp.concatenate([vT, jnp.ones((BF16_ROWS, ck), F32)], axis=0).astype(BF16)

    @pl.when(t < n_lat_tiles)
    def _():
        start = pl.multiple_of(jnp.maximum(t * tq - C_WINDOW, 0), C_WINDOW)
        qpos = t * tq + (lax.broadcasted_iota(jnp.int32, (ck, 2 * tq), 1) & (tq - 1))
        chunks = []
        for c in range((tq + 2 * C_WINDOW) // ck):
            kpos = start + c * ck + lax.broadcasted_iota(jnp.int32, (ck, 2 * tq), 0)
            valid = (jnp.abs(kpos - qpos) <= C_WINDOW) & (kpos < n_lat)
            chunks.append(kv_chunk(start + c * ck) + (valid,))
        chunks.append(kv_chunk(n_lat) + (None,))
        run(chunks)

    @pl.when(t >= n_lat_tiles)
    def _():
        run([kv_chunk(n_lat) + (None,)])


def _win_attn(sink, q, k2, v2, n_lat):
    B, H, _, Lt = q.shape
    tq = TOK_TILE
    npair = k2.shape[1]
    nh = H // npair
    return pl.pallas_call(
        functools.partial(_win_attn_kernel, n_lat_tiles=n_lat // tq, n_lat=n_lat),
        grid=(B, npair, Lt // tq),
        in_specs=[
            pl.BlockSpec(memory_space=pltpu.SMEM),
            pl.BlockSpec((1, nh, HEAD_PAD, tq), lambda b, p, t: (b, p, 0, t)),
            pl.BlockSpec((1, 1, Lt, HEAD_PAD), lambda b, p, t: (b, p, 0, 0)),
            pl.BlockSpec((1, 1, Lt, HEAD_PAD), lambda b, p, t: (b, p, 0, 0)),
        ],
        out_specs=pl.BlockSpec((1, tq, nh * C_HDIM), lambda b, p, t: (b, t, p)),
        out_shape=jax.ShapeDtypeStruct((B, Lt, H * C_HDIM), BF16),
        compiler_params=_cparams(("parallel", "arbitrary", "arbitrary")),
        name="win_attn",
    )(sink, q, k2, v2)


def _pool_lane_tile(z, wp, ps, g0):
    n = z.shape[0]
    row = lax.broadcasted_iota(jnp.int32, z.shape, 0)
    low = lax.broadcasted_iota(jnp.int32, z.shape, 1) < D_GDIM

    def shift_down(a, s):
        return jnp.where(row >= s, pltpu.roll(a, s, 0), 0.0)

    def shift_up(a, s):
        return jnp.where(row < n - s, pltpu.roll(a, n - s, 0), 0.0)

    h_lo, h_hi = D_WINDOWS[g0] // 2, D_WINDOWS[g0 + 1] // 2
    back, fwd, h = z, z, 1
    levels = {1: (z, z)}
    while h < h_hi:
        back = back + shift_down(back, h)
        fwd = fwd + shift_up(fwd, h)
        h *= 2
        levels[h] = (back, fwd)
    back_sel = jnp.where(low, levels[h_lo][0], levels[h_hi][0])
    fwd_sel = jnp.where(low, levels[h_lo][1], levels[h_hi][1])
    win_sum = shift_down(back_sel, 1) + fwd_sel
    half = jnp.where(low, h_lo, h_hi)
    cnt = jnp.minimum(row + half, n) - jnp.maximum(row - half, 0)
    pooled = (win_sum / cnt.astype(F32) - z).astype(BF16)
    return jnp.dot(pooled, wp, preferred_element_type=F32) * ps


def _pool_kernel(pz_ref, wp_ref, ps_ref, o_ref, *, n_lat):
    lt = pz_ref.shape[1]
    for lo, hi in ((0, n_lat), (n_lat, lt)):
        for c in range(D_GROUPS // 2):
            c0, c1 = c * LANES, (c + 1) * LANES
            y = _pool_lane_tile(pz_ref[0, lo:hi, c0:c1], wp_ref[c0:c1, c0:c1], ps_ref[:, c0:c1], 2 * c)
            o_ref[0, lo:hi, c0:c1] = y.astype(o_ref.dtype)


def _pool(pz, wp_bd, p_scale, n_lat):
    B, Lt, W = pz.shape
    return pl.pallas_call(
        functools.partial(_pool_kernel, n_lat=n_lat),
        grid=(B,),
        in_specs=[
            pl.BlockSpec((1, Lt, W), lambda b: (b, 0, 0)),
            pl.BlockSpec(wp_bd.shape, lambda b: (0, 0)),
            pl.BlockSpec(p_scale.shape, lambda b: (0, 0)),
        ],
        out_specs=pl.BlockSpec((1, Lt, W), lambda b: (b, 0, 0)),
        out_shape=jax.ShapeDtypeStruct((B, Lt, W), BF16),
        compiler_params=_cparams(("parallel",)),
        name="pool",
    )(pz, wp_bd, p_scale)


def _mix_ffn_kernel(*refs, d_ff, aliased):
    if aliased:
        refs = refs[1:]
    (x_ref, xp_ref, xn_ref, a_ref, ap_ref, an_ref, b_ref, bp_ref, bn_ref, m_ref, g2_ref, wa_ref, wb_ref,
     wup_ref, cw_ref, cb_ref, wdn_ref, o_ref, a_scr, b_scr, x1_scr, u_scr, act_scr) = refs
    t = pl.program_id(1)
    tm = x_ref.shape[1]
    hx = FFN_HALO
    hm = BF16_ROWS
    rows = tm + 2 * hx
    m = m_ref[0]
    g2 = g2_ref[...]

    for src, prv, nxt, scr in ((a_ref, ap_ref, an_ref, a_scr), (b_ref, bp_ref, bn_ref, b_scr)):
        scr[0:tm] = src[0]
        scr[tm:tm + hm] = nxt[0]
        scr[tm + hm:tm + 2 * hm] = prv[0]
    y = (jnp.dot(a_scr[...], wa_ref[...], preferred_element_type=F32)
         + jnp.dot(b_scr[...], wb_ref[...], preferred_element_type=F32))
    gate1 = m[2:3]
    x1_scr[...] = x_ref[0] + gate1 * y[0:tm]
    x1_next = xn_ref[0] + gate1 * y[tm:tm + hx]
    x1_prev = xp_ref[0] + gate1 * y[tm + 2 * hm - hx:tm + 2 * hm]

    def mod_norm_rows(xv):
        return _mod_norm(xv, g2, m[3:4], m[4:5])

    u_scr[0:tm] = mod_norm_rows(x1_scr[...]).astype(BF16)
    u_next = jnp.where(t == pl.num_programs(1) - 1, 0.0, mod_norm_rows(x1_next))
    u_prev = jnp.where(t == 0, 0.0, mod_norm_rows(x1_prev))
    u_scr[tm:rows] = jnp.concatenate([u_next, u_prev], axis=0).astype(BF16)

    def conv(h, col0, width):
        cw = cw_ref[:, col0:col0 + width]
        out = (pltpu.roll(h, 1, 0) * cw[0:1] + h * cw[1:2] + pltpu.roll(h, rows - 1, 0) * cw[2:3]
               + cb_ref[:, col0:col0 + width])
        return out[0:tm]

    u = u_scr[...]
    for c in range(d_ff // FFN_CHUNK):
        c0 = c * FFN_CHUNK
        gate = conv(jnp.dot(u, wup_ref[:, c0:c0 + FFN_CHUNK], preferred_element_type=F32), c0, FFN_CHUNK)
        val = conv(jnp.dot(u, wup_ref[:, d_ff + c0:d_ff + c0 + FFN_CHUNK], preferred_element_type=F32),
                   d_ff + c0, FFN_CHUNK)
        act_scr[:, c0:c0 + FFN_CHUNK] = (gate * jax.nn.sigmoid(gate) * val).astype(BF16)

    y2 = jnp.dot(act_scr[...], wdn_ref[...], preferred_element_type=F32)
    o_ref[0] = x1_scr[...] + m[5:6] * y2


def _mix_ffn(x, att, br, m, wa, wb, g2, w_up, conv_w, conv_b, w_down, row0, n_rows, tm, ctx_mod, out_rows,
             prev_out=None):
    B, Lt, D = x.shape
    d_ff = w_down.shape[0]
    assert row0 % tm == 0 and n_rows % tm == 0 and tm % BF16_ROWS == 0 and 2 * FFN_HALO == BF16_ROWS
    t0 = row0 // tm

    def tile(w):
        return pl.BlockSpec((1, tm, w), lambda b, t: (b, t0 + t, 0))

    def halo(w, h, prev):
        per, first, n = tm // h, row0 // h, Lt // h
        if prev:
            return pl.BlockSpec((1, h, w), lambda b, t: (b, jnp.maximum(first + t * per - 1, 0), 0))
        return pl.BlockSpec((1, h, w), lambda b, t: (b, jnp.minimum(first + (t + 1) * per, n - 1), 0))

    const = lambda a: pl.BlockSpec(a.shape, lambda b, t: (0, 0))
    once = lambda a: pl.BlockSpec(a.shape, lambda b, t: (0, 0), pipeline_mode=pl.Buffered(1))
    wa_w, wb_w = att.shape[2], br.shape[2]
    aliased = prev_out is not None
    in_specs = [
        tile(D), halo(D, FFN_HALO, True), halo(D, FFN_HALO, False),
        tile(wa_w), halo(wa_w, BF16_ROWS, True), halo(wa_w, BF16_ROWS, False),
        tile(wb_w), halo(wb_w, BF16_ROWS, True), halo(wb_w, BF16_ROWS, False),
        pl.BlockSpec((1, N_MOD, D), (lambda b, t: (B, 0, 0)) if ctx_mod else (lambda b, t: (b, 0, 0))),
        const(g2), once(wa), once(wb), once(w_up), const(conv_w), const(conv_b), once(w_down),
    ]
    args = [x, x, x, att, att, att, br, br, br, m, g2, wa, wb, w_up, conv_w, conv_b, w_down]
    if aliased:
        in_specs = [pl.BlockSpec(memory_space=pl.ANY)] + in_specs
        args = [prev_out] + args
    return pl.pallas_call(
        functools.partial(_mix_ffn_kernel, d_ff=d_ff, aliased=aliased),
        grid=(B, n_rows // tm),
        in_specs=in_specs,
        out_specs=pl.BlockSpec((1, tm, D), lambda b, t: (b, t0 + t, 0)),
        out_shape=jax.ShapeDtypeStruct((B, out_rows, D), F32),
        scratch_shapes=[
            pltpu.VMEM((tm + 2 * BF16_ROWS, wa_w), BF16), pltpu.VMEM((tm + 2 * BF16_ROWS, wb_w), BF16),
            pltpu.VMEM((tm, D), F32), pltpu.VMEM((tm + 2 * FFN_HALO, D), BF16), pltpu.VMEM((tm, d_ff), BF16)],
        input_output_aliases={0: 0} if aliased else {},
        compiler_params=_cparams(("parallel", "arbitrary")),
        name="mix_ffn",
    )(*args)


def _rope_tables_T(rows, rot_dim, n_ctx):
    n_freq = rot_dim // 4
    inv = ROPE_BASE ** (-jnp.arange(n_freq, dtype=F32) / n_freq)
    row = jnp.broadcast_to(jnp.arange(rows, dtype=F32)[:, None], (rows, GRID_W)).reshape(-1)
    col = jnp.broadcast_to(jnp.arange(GRID_W, dtype=F32)[None, :], (rows, GRID_W)).reshape(-1)
    ang = jnp.concatenate([row[:, None] * inv, col[:, None] * inv], axis=-1)
    cosT = jnp.concatenate([jnp.cos(ang).T, jnp.ones((rot_dim // 2, n_ctx), F32)], axis=1)
    sinT = jnp.concatenate([jnp.sin(ang).T, jnp.zeros((rot_dim // 2, n_ctx), F32)], axis=1)
    return cosT, sinT


def _dft_tables(n):
    idx = (jnp.arange(n, dtype=jnp.int32)[:, None] * jnp.arange(n, dtype=jnp.int32)[None, :]) % n
    ang = idx.astype(F32) * (2.0 * math.pi / n)
    return jnp.cos(ang), jnp.sin(ang)


def _dft_tables_split(n, r):
    l = jnp.arange(n, dtype=jnp.int32)[:, None]
    kk = jnp.arange(r, dtype=jnp.int32)[None, :]
    a = ((kk * l) % r).astype(F32) * (2.0 * math.pi / r)
    b = ((kk * l) % n).astype(F32) * (2.0 * math.pi / n)
    ca, sa, cb, sb = jnp.cos(a)[:, :, None], jnp.sin(a)[:, :, None], jnp.cos(b)[:, None, :], jnp.sin(b)[:, None, :]
    return (ca * cb - sa * sb).reshape(n, n), (sa * cb + ca * sb).reshape(n, n)


def _block_diag(w):
    g, a, b = w.shape
    eye = jnp.eye(g, dtype=w.dtype)
    return (eye[:, None, :, None] * w[:, :, None, :]).reshape(g * a, g * b)


def _col(v):
    return v.reshape(-1, 1).astype(F32)


def kernel(x, c, ctx, c_ctx, mod_w, mod_b, norm1_g, norm2_g, mla_w_in, mla_cq_g, mla_ckv_g, mla_w_uq, mla_w_ukv,
           mla_q_g, mla_k_g, fnet_w, even_w_out, win_w_in, win_q_g, win_k_g, win_sink, pool_w, pool_scale, odd_w_out,
           ffn_up, ffn_conv_w, ffn_conv_b, ffn_down):
    B, L, D = x.shape
    C = ctx.shape[1]
    depth = mod_w.shape[0]
    assert L % TOK_TILE == 0 and C == TOK_TILE and B % FNET_BGRP == 0
    n_lat_tiles = L // TOK_TILE
    rows = L // GRID_W

    pad = (-(B + 1)) % 8
    cc = jnp.concatenate([c, c_ctx[None, :], jnp.zeros((pad, D), F32)], axis=0)
    mods = _modulation(cc, mod_w, mod_b).reshape(depth, B + 1 + pad, N_MOD, D)

    cos_a, sin_a = _rope_tables_T(rows, A_ROPE, C)
    cos_w, sin_w = _rope_tables_T(rows, C_HDIM, C)
    r_lat = math.isqrt(L)
    c_lat, s_lat = (t.astype(BF16) for t in (_dft_tables_split(L, r_lat) if r_lat * r_lat == L else _dft_tables(L)))
    c_ctx_t, s_ctx_t = (t.astype(BF16) for t in _dft_tables(C))
    cc64, ss64 = _dft_tables(B_GDIM)
    eye_g = jnp.eye(B_GROUPS, dtype=F32)
    cs_bd = jnp.concatenate([jnp.kron(eye_g, cc64), -jnp.kron(eye_g, ss64)], axis=0).astype(BF16)

    nh = C_HEADS // (C_KV_HEADS // 2)
    head_order = [p * nh + e * (nh // 2) + j for p in range(C_KV_HEADS // 2) for j in range(nh // 2) for e in range(2)]
    att_perm = jnp.asarray([h * C_HDIM + d for h in head_order for d in range(C_HDIM)], jnp.int32)

    h = jnp.concatenate([x, ctx], axis=1)
    for i in range(depth):
        j = i // 2
        m = mods[i]
        g1 = norm1_g[i][None, :]
        if i % 2 == 0:
            q, kT, vT, z = _even_in(
                h, m, g1, mla_w_in[j].T.astype(BF16), _col(mla_cq_g[j]), mla_w_uq[j].T.astype(BF16),
                _col(mla_q_g[j]), _col(mla_ckv_g[j]), mla_w_ukv[j].T.astype(BF16), _col(mla_k_g[j]),
                cos_a, sin_a, n_lat_tiles)
            att = _mla_attn(q, kT, vT, 0, L, MLA_Q_TILE, 0, L + C)
            att = _mla_attn(q, kT, vT, L, C, C, L, C, prev_out=att)
            wf_bd = _block_diag(fnet_w[j]).astype(BF16)
            br = _fnet(z, c_lat, s_lat, cs_bd, wf_bd, 0, L + C)
            br = _fnet(z, c_ctx_t, s_ctx_t, cs_bd, wf_bd, L, L + C, prev_out=br)
            w_out = even_w_out[j].astype(BF16)
            wa, wb = w_out[:A_HEADS * A_V], w_out[A_HEADS * A_V:]
        else:
            q, k2, v2, pz = _odd_in(h, m, g1, win_w_in[j].T.astype(BF16), _col(win_q_g[j]), _col(win_k_g[j]),
                                    cos_w, sin_w, n_lat_tiles)
            att = _win_attn(win_sink[j].reshape(-1), q, k2, v2, L)
            br = _pool(pz, _block_diag(pool_w[j]).astype(BF16), pool_scale[j][None, :], L)
            w_out = odd_w_out[j].astype(BF16)
            wa, wb = w_out[:C_QW][att_perm], w_out[C_QW:]
        ffn_args = (att, br, m, wa, wb, norm2_g[i][None, :], ffn_up[i].astype(BF16), ffn_conv_w[i],
                    ffn_conv_b[i][None, :], ffn_down[i].astype(BF16))
        h_in = h
        if i < depth - 1:
            h = _mix_ffn(h_in, *ffn_args, 0, L, FFN_LAT_TILE, False, L + C)
            h = _mix_ffn(h_in, *ffn_args, L, C, C, True, L + C, prev_out=h)
        else:
            h = _mix_ffn(h_in, *ffn_args, 0, L, FFN_LAT_TILE, False, L)
    return h
```

```python
import functools
import math

import jax
import jax.numpy as jnp
from jax import lax
from jax.experimental import pallas as pl
from jax.experimental.pallas import tpu as pltpu

F32 = jnp.float32
BF16 = jnp.bfloat16

GRID_W = 64
N_MOD = 6
EPS = 1e-6
ROPE_BASE = 10000.0
NEG_INF = -1e30
A_HEADS = 12
A_NOPE = 64
A_ROPE = 32
A_QK = A_NOPE + A_ROPE
A_V = 64
A_Q_LORA = 256
A_KV_LORA = 128
A_IN = A_Q_LORA + A_KV_LORA + A_ROPE
B_GROUPS = 4
B_GDIM = 64
B_WIDTH = B_GROUPS * B_GDIM
C_HEADS = 12
C_KV_HEADS = 4
C_GROUP = C_HEADS // C_KV_HEADS
C_HDIM = 64
C_WINDOW = 128
C_QW = C_HEADS * C_HDIM
C_KW = C_KV_HEADS * C_HDIM
D_GROUPS = 4
D_GDIM = 64
D_WIDTH = D_GROUPS * D_GDIM
D_WINDOWS = (2, 4, 8, 16)

LANES = 128
BF16_ROWS = 16
HEAD_PAD = LANES
TOK_TILE = 256
FFN_HALO = 8
FFN_CHUNK = 256
FFN_LAT_TILE = 1024
FNET_BGRP = 4
IN_TILE = 512
MLA_Q_TILE = 512
MLA_KEY_CHUNK = 256
MLA_PV_LAG = 1
LOG2_E = math.log2(math.e)
WIN_KEY_CHUNK = 256
VMEM_LIMIT = 56 * 1024 * 1024

NT_DIMS = (((1,), (1,)), ((), ()))


def _cparams(sem):
    return pltpu.CompilerParams(dimension_semantics=sem, vmem_limit_bytes=VMEM_LIMIT)


def _mod_norm(x, g, shift, scale):
    ms = jnp.mean(x * x, axis=-1, keepdims=True)
    y = x * lax.rsqrt(ms + EPS) * g
    return y * (1.0 + scale) + shift


def _rms_rows(xT, g_col):
    ms = jnp.mean(xT * xT, axis=0, keepdims=True)
    return xT * lax.rsqrt(ms + EPS) * g_col


def _rope_rows(x1, x2, cos, sin):
    return x1 * cos - x2 * sin, x1 * sin + x2 * cos


def _mod_kernel(cc_ref, w_ref, b_ref, o_ref):
    cc = cc_ref[...]
    a = (cc * jax.nn.sigmoid(cc)).astype(BF16)
    w = w_ref[0].astype(BF16)
    o_ref[0] = jnp.dot(a, w, preferred_element_type=F32) + b_ref[0]


def _modulation(cc, mod_w, mod_b):
    depth, d, n = mod_w.shape
    rows = cc.shape[0]
    tn = 1024
    return pl.pallas_call(
        _mod_kernel,
        grid=(depth, n // tn),
        in_specs=[
            pl.BlockSpec((rows, d), lambda i, j: (0, 0)),
            pl.BlockSpec((1, d, tn), lambda i, j: (i, 0, j)),
            pl.BlockSpec((1, 1, tn), lambda i, j: (i, 0, j)),
        ],
        out_specs=pl.BlockSpec((1, rows, tn), lambda i, j: (i, 0, j)),
        out_shape=jax.ShapeDtypeStruct((depth, rows, n), F32),
        compiler_params=_cparams(("arbitrary", "arbitrary")),
        name="modulation",
    )(cc, mod_w, mod_b.reshape(depth, 1, n))


def _even_in_kernel(x_ref, m_ref, g1_ref, w_in_ref, cqg_ref, wuq_ref, qg_ref, ckvg_ref, wukv_ref, kg_ref,
                    cos_ref, sin_ref, q_ref, k_ref, v_ref, z_ref):
    tm = x_ref.shape[1]
    m = m_ref[0]
    u = _mod_norm(x_ref[0], g1_ref[...], m[0:1], m[1:2]).astype(BF16)
    hT = lax.dot_general(w_in_ref[...], u, NT_DIMS, preferred_element_type=F32)
    cos = cos_ref[...]
    sin = sin_ref[...]
    scale = A_QK ** -0.5 * LOG2_E
    half = A_ROPE // 2
    zpad = jnp.zeros((HEAD_PAD - A_QK, tm), F32)
    ones_rows = jnp.ones((BF16_ROWS, tm), F32)

    z_ref[0] = hT[A_IN:A_IN + B_WIDTH].T.astype(BF16)

    cqn = _rms_rows(hT[0:A_Q_LORA], cqg_ref[...]).astype(BF16)
    qT = jnp.dot(wuq_ref[...], cqn, preferred_element_type=F32)
    qg = qg_ref[...] * scale
    for h in range(A_HEADS):
        qn = _rms_rows(qT[h * A_QK:(h + 1) * A_QK], qg)
        o1, o2 = _rope_rows(qn[A_NOPE:A_NOPE + half], qn[A_NOPE + half:A_QK], cos, sin)
        full = jnp.concatenate([qn[0:A_NOPE], o1, o2, zpad], axis=0)
        q_ref[0, h] = full.astype(BF16)

    ckvn = _rms_rows(hT[A_Q_LORA:A_Q_LORA + A_KV_LORA], ckvg_ref[...]).astype(BF16)
    kvT = jnp.dot(wukv_ref[...], ckvn, preferred_element_type=F32)
    kr = hT[A_Q_LORA + A_KV_LORA:A_IN]
    kr_ss = jnp.sum(kr * kr, axis=0, keepdims=True)
    kg = kg_ref[...]
    hw = A_NOPE + A_V
    for h in range(A_HEADS):
        kn = kvT[h * hw:h * hw + A_NOPE]
        r = lax.rsqrt((jnp.sum(kn * kn, axis=0, keepdims=True) + kr_ss) * (1.0 / A_QK) + EPS)
        knn = kn * r * kg[0:A_NOPE]
        krn = kr * r * kg[A_NOPE:A_QK]
        o1, o2 = _rope_rows(krn[0:half], krn[half:A_ROPE], cos, sin)
        k_ref[0, h] = jnp.concatenate([knn, o1, o2, zpad], axis=0).T.astype(BF16)
        vh = kvT[h * hw + A_NOPE:(h + 1) * hw]
        v_ref[0, h] = jnp.concatenate([vh, ones_rows], axis=0).astype(BF16)


def _even_in(x, m, g1, w_inT, cq_g, w_uqT, q_g, ckv_g, w_ukvT, k_g, cosT, sinT, n_lat_tiles):
    B, Lt, D = x.shape
    tm = IN_TILE
    nt = pl.cdiv(Lt, tm)
    n_lat_tiles = (n_lat_tiles * TOK_TILE) // tm
    full2 = lambda a: pl.BlockSpec(a.shape, lambda b, t: (0, 0))
    return pl.pallas_call(
        _even_in_kernel,
        grid=(B, nt),
        in_specs=[
            pl.BlockSpec((1, tm, D), lambda b, t: (b, t, 0)),
            pl.BlockSpec((1, N_MOD, D), lambda b, t: (jnp.where(t < n_lat_tiles, b, B), 0, 0)),
            full2(g1), full2(w_inT), full2(cq_g), full2(w_uqT), full2(q_g), full2(ckv_g), full2(w_ukvT), full2(k_g),
            pl.BlockSpec((A_ROPE // 2, tm), lambda b, t: (0, t)),
            pl.BlockSpec((A_ROPE // 2, tm), lambda b, t: (0, t)),
        ],
        out_specs=[
            pl.BlockSpec((1, A_HEADS, HEAD_PAD, tm), lambda b, t: (b, 0, 0, t)),
            pl.BlockSpec((1, A_HEADS, tm, HEAD_PAD), lambda b, t: (b, 0, t, 0)),
            pl.BlockSpec((1, A_HEADS, A_V + BF16_ROWS, tm), lambda b, t: (b, 0, 0, t)),
            pl.BlockSpec((1, tm, B_WIDTH), lambda b, t: (b, t, 0)),
        ],
        out_shape=[
            jax.ShapeDtypeStruct((B, A_HEADS, HEAD_PAD, Lt), BF16),
            jax.ShapeDtypeStruct((B, A_HEADS, Lt, HEAD_PAD), BF16),
            jax.ShapeDtypeStruct((B, A_HEADS, A_V + BF16_ROWS, Lt), BF16),
            jax.ShapeDtypeStruct((B, Lt, B_WIDTH), BF16),
        ],
        compiler_params=_cparams(("parallel", "arbitrary")),
        name="even_in",
    )(x, m, g1, w_inT, cq_g, w_uqT, q_g, ckv_g, w_ukvT, k_g, cosT, sinT)


def _mla_attn_kernel(*refs, aliased):
    q_ref, k_ref, v_ref, o_ref = refs[1:] if aliased else refs
    nk = k_ref.shape[2]
    qs = [q_ref[0, j] for j in range(2)]
    m = [None, None]
    acc = [None, None]
    pend = [[], []]
    chunks = [(k0, min(k0 + MLA_KEY_CHUNK, nk)) for k0 in range(0, nk, MLA_KEY_CHUNK)]

    def scores(j, k0, k1):
        s = jnp.dot(k_ref[0, j, k0:k1, :], qs[j], preferred_element_type=F32)
        mx = jnp.max(s, axis=0, keepdims=True)
        m_new = mx if m[j] is None else jnp.maximum(m[j], mx)
        alpha = None if m[j] is None else jnp.exp2(m[j] - m_new)
        pend[j].append((jnp.exp2(s - m_new).astype(BF16), alpha, k0, k1))
        m[j] = m_new

    def weighted_values(j):
        p, alpha, k0, k1 = pend[j].pop(0)
        pv = jnp.dot(v_ref[0, j, :, k0:k1], p, preferred_element_type=F32)
        acc[j] = pv if alpha is None else alpha * acc[j] + pv

    for i in range(len(chunks) + MLA_PV_LAG):
        for j in range(2):
            if i < len(chunks):
                scores(j, *chunks[i])
            if i >= MLA_PV_LAG:
                weighted_values(j)
    outs = [(a / a[A_V:A_V + 1])[0:A_V] for a in acc]
    o_ref[0] = jnp.concatenate(outs, axis=0).T.astype(o_ref.dtype)


def _mla_attn(q, kT, vT, q_row0, n_q, tq, k_col0, n_k, prev_out=None):
    B, H, _, Lt = q.shape
    assert q_row0 % tq == 0 and n_q % tq == 0 and k_col0 % n_k == 0
    qb, kb = q_row0 // tq, k_col0 // n_k
    aliased = prev_out is not None
    in_specs = [
        pl.BlockSpec((1, 2, HEAD_PAD, tq), lambda b, h, t: (b, h, 0, qb + t)),
        pl.BlockSpec((1, 2, n_k, HEAD_PAD), lambda b, h, t: (b, h, kb, 0)),
        pl.BlockSpec((1, 2, A_V + BF16_ROWS, n_k), lambda b, h, t: (b, h, 0, kb)),
    ]
    args = [q, kT, vT]
    if aliased:
        in_specs = [pl.BlockSpec(memory_space=pl.ANY)] + in_specs
        args = [prev_out] + args
    return pl.pallas_call(
        functools.partial(_mla_attn_kernel, aliased=aliased),
        grid=(B, H // 2, n_q // tq),
        in_specs=in_specs,
        out_specs=pl.BlockSpec((1, tq, 2 * A_V), lambda b, h, t: (b, qb + t, h)),
        out_shape=jax.ShapeDtypeStruct((B, Lt, H * A_V), BF16),
        input_output_aliases={0: 0} if aliased else {},
        compiler_params=_cparams(("parallel", "arbitrary", "arbitrary")),
        name="mla_attn",
    )(*args)


def _fnet_kernel(*refs, norm, aliased):
    if aliased:
        refs = refs[1:]
    c_ref, s_ref, z_ref, cs_ref, wf_ref, o_ref, p_acc, q_acc = refs
    k = pl.program_id(2)

    @pl.when(k == 0)
    def _():
        p_acc[...] = jnp.zeros_like(p_acc)
        q_acc[...] = jnp.zeros_like(q_acc)

    c = c_ref[...]
    s = s_ref[...]
    for j in range(z_ref.shape[0]):
        z = z_ref[j]
        p_acc[j] += jnp.dot(c, z, preferred_element_type=F32)
        q_acc[j] += jnp.dot(s, z, preferred_element_type=F32)

    @pl.when(k == pl.num_programs(2) - 1)
    def _():
        for j in range(z_ref.shape[0]):
            pq = jnp.concatenate([p_acc[j], q_acc[j]], axis=1).astype(BF16)
            y = jnp.dot(pq, cs_ref[...], preferred_element_type=F32) * norm
            o_ref[j] = jnp.dot(y.astype(BF16), wf_ref[...], preferred_element_type=F32).astype(o_ref.dtype)


def _fnet_fold_kernel(c_ref, s_ref, z_ref, zr_ref, zmid_ref, cs_ref, wf_ref, o_ref, p_acc, q_acc, *, norm):
    i = pl.program_id(1)
    k = pl.program_id(2)
    tm, tk = c_ref.shape

    @pl.when(k == 0)
    def _():
        p_acc[...] = jnp.zeros_like(p_acc)
        q_acc[...] = jnp.zeros_like(q_acc)

    c = c_ref[...]
    s = s_ref[...]
    unpaired = (k * tk + lax.broadcasted_iota(jnp.int32, (tk, 1), 0)) == 0
    for j in range(z_ref.shape[0]):
        z = z_ref[j].astype(F32)
        zr = zr_ref[j].astype(F32)
        p_acc[j] += jnp.dot(c, (z + jnp.where(unpaired, 0.0, zr)).astype(BF16), preferred_element_type=F32)
        q_acc[j] += jnp.dot(s, (z - zr).astype(BF16), preferred_element_type=F32)

    @pl.when(k == pl.num_programs(2) - 1)
    def _():
        odd = (i * tm + lax.broadcasted_iota(jnp.int32, (tm, 1), 0)) & 1
        sign = (1 - 2 * odd).astype(F32)
        for j in range(z_ref.shape[0]):
            p = p_acc[j] + sign * zmid_ref[j, 0:1, :].astype(F32)
            pq = jnp.concatenate([p, q_acc[j]], axis=1).astype(BF16)
            y = jnp.dot(pq, cs_ref[...], preferred_element_type=F32) * norm
            o_ref[j] = jnp.dot(y.astype(BF16), wf_ref[...], preferred_element_type=F32).astype(o_ref.dtype)


def _fnet_fold(z, zr, c_tab, s_tab, cs_bd, wf_bd, out_rows):
    B = z.shape[0]
    n = c_tab.shape[0]
    tm = tk = min(n // 2, 1024)
    g = FNET_BGRP
    mid = BF16_ROWS
    return pl.pallas_call(
        functools.partial(_fnet_fold_kernel, norm=1.0 / math.sqrt(n * B_GDIM)),
        grid=(B // g, n // tm, n // 2 // tk),
        in_specs=[
            pl.BlockSpec((tm, tk), lambda b, i, k: (i, k)),
            pl.BlockSpec((tm, tk), lambda b, i, k: (i, k)),
            pl.BlockSpec((g, tk, B_WIDTH), lambda b, i, k: (b, k, 0)),
            pl.BlockSpec((g, tk, B_WIDTH), lambda b, i, k: (b, k, 0)),
            pl.BlockSpec((g, mid, B_WIDTH), lambda b, i, k: (b, n // 2 // mid, 0)),
            pl.BlockSpec(cs_bd.shape, lambda b, i, k: (0, 0)),
            pl.BlockSpec(wf_bd.shape, lambda b, i, k: (0, 0)),
        ],
        out_specs=pl.BlockSpec((g, tm, B_WIDTH), lambda b, i, k: (b, i, 0)),
        out_shape=jax.ShapeDtypeStruct((B, out_rows, B_WIDTH), BF16),
        scratch_shapes=[pltpu.VMEM((g, tm, B_WIDTH), F32), pltpu.VMEM((g, tm, B_WIDTH), F32)],
        compiler_params=_cparams(("parallel", "arbitrary", "arbitrary")),
        name="fnet_fold",
    )(c_tab, s_tab, z, zr, z, cs_bd, wf_bd)


def _fnet(z, c_tab, s_tab, cs_bd, wf_bd, row0, out_rows, prev_out=None):
    B = z.shape[0]
    n = c_tab.shape[0]
    tm = min(n, 1024)
    tk = min(n, 1024)
    g = FNET_BGRP
    norm = 1.0 / math.sqrt(n * B_GDIM)
    aliased = prev_out is not None
    in_specs = [
        pl.BlockSpec((tm, tk), lambda b, i, k: (i, k)),
        pl.BlockSpec((tm, tk), lambda b, i, k: (i, k)),
        pl.BlockSpec((g, tk, B_WIDTH), lambda b, i, k: (b, row0 // tk + k, 0)),
        pl.BlockSpec(cs_bd.shape, lambda b, i, k: (0, 0)),
        pl.BlockSpec(wf_bd.shape, lambda b, i, k: (0, 0)),
    ]
    args = [c_tab, s_tab, z, cs_bd, wf_bd]
    if aliased:
        in_specs = [pl.BlockSpec(memory_space=pl.ANY)] + in_specs
        args = [prev_out] + args
    return pl.pallas_call(
        functools.partial(_fnet_kernel, norm=norm, aliased=aliased),
        grid=(B // g, n // tm, n // tk),
        in_specs=in_specs,
        out_specs=pl.BlockSpec((g, tm, B_WIDTH), lambda b, i, k: (b, row0 // tm + i, 0)),
        out_shape=jax.ShapeDtypeStruct((B, out_rows, B_WIDTH), BF16),
        scratch_shapes=[pltpu.VMEM((g, tm, B_WIDTH), F32), pltpu.VMEM((g, tm, B_WIDTH), F32)],
        input_output_aliases={0: 0} if aliased else {},
        compiler_params=_cparams(("parallel", "arbitrary", "arbitrary")),
        name="fnet",
    )(*args)


def _odd_in_kernel(x_ref, m_ref, g1_ref, w_in_ref, qg_ref, kg_ref, cos_ref, sin_ref, q_ref, k_ref, v_ref, pz_ref):
    tm = x_ref.shape[1]
    m = m_ref[0]
    u = _mod_norm(x_ref[0], g1_ref[...], m[0:1], m[1:2]).astype(BF16)
    hT = lax.dot_general(w_in_ref[...], u, NT_DIMS, preferred_element_type=F32)
    cos = cos_ref[...]
    sin = sin_ref[...]
    half = C_HDIM // 2
    zeros = jnp.zeros((C_HDIM, tm), F32)

    def norm_rope(xT, g_col):
        xn = _rms_rows(xT, g_col)
        o1, o2 = _rope_rows(xn[0:half], xn[half:C_HDIM], cos, sin)
        return jnp.concatenate([o1, o2], axis=0)

    qg = qg_ref[...] * (C_HDIM ** -0.5 * LOG2_E)
    for h in range(C_HEADS):
        qh = norm_rope(hT[h * C_HDIM:(h + 1) * C_HDIM], qg)
        parts = [qh, zeros] if (h // C_GROUP) % 2 == 0 else [zeros, qh]
        q_ref[0, h] = jnp.concatenate(parts, axis=0).astype(BF16)

    kg = kg_ref[...]
    for p in range(C_KV_HEADS // 2):
        ks = [norm_rope(hT[C_QW + (2 * p + e) * C_HDIM:C_QW + (2 * p + e + 1) * C_HDIM], kg) for e in range(2)]
        k_ref[0, p] = jnp.concatenate(ks, axis=0).T.astype(BF16)
        v0 = C_QW + C_KW + 2 * p * C_HDIM
        v_ref[0, p] = hT[v0:v0 + 2 * C_HDIM].T.astype(BF16)

    pz_ref[0] = hT[C_QW + 2 * C_KW:C_QW + 2 * C_KW + D_WIDTH].T


def _odd_in(x, m, g1, w_inT, q_g, k_g, cosT, sinT, n_lat_tiles):
    B, Lt, D = x.shape
    tm = TOK_TILE
    nt = Lt // tm
    full2 = lambda a: pl.BlockSpec(a.shape, lambda b, t: (0, 0))
    np_ = C_KV_HEADS // 2
    return pl.pallas_call(
        _odd_in_kernel,
        grid=(B, nt),
        in_specs=[
            pl.BlockSpec((1, tm, D), lambda b, t: (b, t, 0)),
            pl.BlockSpec((1, N_MOD, D), lambda b, t: (jnp.where(t < n_lat_tiles, b, B), 0, 0)),
            full2(g1), full2(w_inT), full2(q_g), full2(k_g),
            pl.BlockSpec((C_HDIM // 2, tm), lambda b, t: (0, t)),
            pl.BlockSpec((C_HDIM // 2, tm), lambda b, t: (0, t)),
        ],
        out_specs=[
            pl.BlockSpec((1, C_HEADS, HEAD_PAD, tm), lambda b, t: (b, 0, 0, t)),
            pl.BlockSpec((1, np_, tm, HEAD_PAD), lambda b, t: (b, 0, t, 0)),
            pl.BlockSpec((1, np_, tm, HEAD_PAD), lambda b, t: (b, 0, t, 0)),
            pl.BlockSpec((1, tm, D_WIDTH), lambda b, t: (b, t, 0)),
        ],
        out_shape=[
            jax.ShapeDtypeStruct((B, C_HEADS, HEAD_PAD, Lt), BF16),
            jax.ShapeDtypeStruct((B, np_, Lt, HEAD_PAD), BF16),
            jax.ShapeDtypeStruct((B, np_, Lt, HEAD_PAD), BF16),
            jax.ShapeDtypeStruct((B, Lt, D_WIDTH), F32),
        ],
        compiler_params=_cparams(("parallel", "arbitrary")),
        name="odd_in",
    )(x, m, g1, w_inT, q_g, k_g, cosT, sinT)


def _win_attn_kernel(sink_ref, q_ref, k_ref, v_ref, o_ref, *, n_lat_tiles, n_lat):
    pair = pl.program_id(1)
    t = pl.program_id(2)
    nh = q_ref.shape[1]
    tq = q_ref.shape[3]
    half = nh // 2
    ck = WIN_KEY_CHUNK
    col = lax.broadcasted_iota(jnp.int32, (1, 2 * tq), 1)
    row = lax.broadcasted_iota(jnp.int32, (HEAD_PAD, tq), 0)

    def run(chunks):
        qg = [jnp.concatenate([q_ref[0, j], q_ref[0, half + j]], axis=1) for j in range(half)]
        sk = [jnp.where(col < tq, sink_ref[pair * nh + j], sink_ref[pair * nh + half + j]) * LOG2_E
              for j in range(half)]
        m = list(sk)
        acc = [None] * half
        pend = [[] for _ in range(half)]

        def scores(j, kb, valid):
            s = jnp.dot(kb, qg[j], preferred_element_type=F32)
            if valid is not None:
                s = jnp.where(valid, s, NEG_INF)
            m_new = jnp.maximum(m[j], jnp.max(s, axis=0, keepdims=True))
            pend[j].append((jnp.exp2(s - m_new).astype(BF16), jnp.exp2(m[j] - m_new)))
            m[j] = m_new

        def weighted_values(j, vbT):
            p, alpha = pend[j].pop(0)
            pv = jnp.dot(vbT, p, preferred_element_type=F32)
            acc[j] = pv if acc[j] is None else alpha * acc[j] + pv

        for i in range(len(chunks) + 1):
            for j in range(half):
                if i < len(chunks):
                    scores(j, chunks[i][0], chunks[i][2])
                if i >= 1:
                    weighted_values(j, chunks[i - 1][1])
        for j in range(half):
            l = acc[j][HEAD_PAD:HEAD_PAD + 1] + jnp.exp2(sk[j] - m[j])
            o = acc[j][0:HEAD_PAD] / l
            blk = jnp.where(row < C_HDIM, o[:, 0:tq], o[:, tq:2 * tq])
            o_ref[0, :, j * HEAD_PAD:(j + 1) * HEAD_PAD] = blk.T.astype(o_ref.dtype)

    def kv_chunk(start):
        kb = k_ref[0, 0, pl.ds(start, ck), :]
        vT = v_ref[0, 0, pl.ds(start, ck), :].astype(F32).T
        return kb, jnp.concatenate([vT, jnp.ones((BF16_ROWS, ck), F32)], axis=0).astype(BF16)

    @pl.when(t < n_lat_tiles)
    def _():
        start = pl.multiple_of(jnp.maximum(t * tq - C_WINDOW, 0), C_WINDOW)
        qpos = t * tq + (lax.broadcasted_iota(jnp.int32, (ck, 2 * tq), 1) & (tq - 1))
        chunks = []
        for c in range((tq + 2 * C_WINDOW) // ck):
            kpos = start + c * ck + lax.broadcasted_iota(jnp.int32, (ck, 2 * tq), 0)
            valid = (jnp.abs(kpos - qpos) <= C_WINDOW) & (kpos < n_lat)
            chunks.append(kv_chunk(start + c * ck) + (valid,))
        chunks.append(kv_chunk(n_lat) + (None,))
        run(chunks)

    @pl.when(t >= n_lat_tiles)
    def _():
        run([kv_chunk(n_lat) + (None,)])


def _win_attn(sink, q, k2, v2, n_lat):
    B, H, _, Lt = q.shape
    tq = TOK_TILE
    npair = k2.shape[1]
    nh = H // npair
    return pl.pallas_call(
        functools.partial(_win_attn_kernel, n_lat_tiles=n_lat // tq, n_lat=n_lat),
        grid=(B, npair, Lt // tq),
        in_specs=[
            pl.BlockSpec(memory_space=pltpu.SMEM),
            pl.BlockSpec((1, nh, HEAD_PAD, tq), lambda b, p, t: (b, p, 0, t)),
            pl.BlockSpec((1, 1, Lt, HEAD_PAD), lambda b, p, t: (b, p, 0, 0)),
            pl.BlockSpec((1, 1, Lt, HEAD_PAD), lambda b, p, t: (b, p, 0, 0)),
        ],
        out_specs=pl.BlockSpec((1, tq, nh * C_HDIM), lambda b, p, t: (b, t, p)),
        out_shape=jax.ShapeDtypeStruct((B, Lt, H * C_HDIM), BF16),
        compiler_params=_cparams(("parallel", "arbitrary", "arbitrary")),
        name="win_attn",
    )(sink, q, k2, v2)


def _pool_lane_tile(z, wp, ps, g0):
    n = z.shape[0]
    row = lax.broadcasted_iota(jnp.int32, z.shape, 0)
    low = lax.broadcasted_iota(jnp.int32, z.shape, 1) < D_GDIM

    def shift_down(a, s):
        return jnp.where(row >= s, pltpu.roll(a, s, 0), 0.0)

    def shift_up(a, s):
        return jnp.where(row < n - s, pltpu.roll(a, n - s, 0), 0.0)

    h_lo, h_hi = D_WINDOWS[g0] // 2, D_WINDOWS[g0 + 1] // 2
    back, fwd, h = z, z, 1
    levels = {1: (z, z)}
    while h < h_hi:
        back = back + shift_down(back, h)
        fwd = fwd + shift_up(fwd, h)
        h *= 2
        levels[h] = (back, fwd)
    back_sel = jnp.where(low, levels[h_lo][0], levels[h_hi][0])
    fwd_sel = jnp.where(low, levels[h_lo][1], levels[h_hi][1])
    win_sum = shift_down(back_sel, 1) + fwd_sel
    half = jnp.where(low, h_lo, h_hi)
    cnt = jnp.minimum(row + half, n) - jnp.maximum(row - half, 0)
    pooled = (win_sum / cnt.astype(F32) - z).astype(BF16)
    return jnp.dot(pooled, wp, preferred_element_type=F32) * ps


def _pool_kernel(pz_ref, wp_ref, ps_ref, o_ref, *, n_lat):
    lt = pz_ref.shape[1]
    for lo, hi in ((0, n_lat), (n_lat, lt)):
        for c in range(D_GROUPS // 2):
            c0, c1 = c * LANES, (c + 1) * LANES
            y = _pool_lane_tile(pz_ref[0, lo:hi, c0:c1], wp_ref[c0:c1, c0:c1], ps_ref[:, c0:c1], 2 * c)
            o_ref[0, lo:hi, c0:c1] = y.astype(o_ref.dtype)


def _pool(pz, wp_bd, p_scale, n_lat):
    B, Lt, W = pz.shape
    return pl.pallas_call(
        functools.partial(_pool_kernel, n_lat=n_lat),
        grid=(B,),
        in_specs=[
            pl.BlockSpec((1, Lt, W), lambda b: (b, 0, 0)),
            pl.BlockSpec(wp_bd.shape, lambda b: (0, 0)),
            pl.BlockSpec(p_scale.shape, lambda b: (0, 0)),
        ],
        out_specs=pl.BlockSpec((1, Lt, W), lambda b: (b, 0, 0)),
        out_shape=jax.ShapeDtypeStruct((B, Lt, W), BF16),
        compiler_params=_cparams(("parallel",)),
        name="pool",
    )(pz, wp_bd, p_scale)


def _mix_ffn_kernel(*refs, d_ff, aliased):
    if aliased:
        refs = refs[1:]
    (x_ref, xp_ref, xn_ref, a_ref, ap_ref, an_ref, b_ref, bp_ref, bn_ref, m_ref, g2_ref, wa_ref, wb_ref,
     wup_ref, cw_ref, cb_ref, wdn_ref, o_ref, a_scr, b_scr, x1_scr, u_scr, act_scr) = refs
    t = pl.program_id(1)
    tm = x_ref.shape[1]
    hx = FFN_HALO
    hm = BF16_ROWS
    rows = tm + 2 * hx
    m = m_ref[0]
    g2 = g2_ref[...]

    for src, prv, nxt, scr in ((a_ref, ap_ref, an_ref, a_scr), (b_ref, bp_ref, bn_ref, b_scr)):
        scr[0:tm] = src[0]
        scr[tm:tm + hm] = nxt[0]
        scr[tm + hm:tm + 2 * hm] = prv[0]
    y = (jnp.dot(a_scr[...], wa_ref[...], preferred_element_type=F32)
         + jnp.dot(b_scr[...], wb_ref[...], preferred_element_type=F32))
    gate1 = m[2:3]
    x1_scr[...] = x_ref[0] + gate1 * y[0:tm]
    x1_next = xn_ref[0] + gate1 * y[tm:tm + hx]
    x1_prev = xp_ref[0] + gate1 * y[tm + 2 * hm - hx:tm + 2 * hm]

    def mod_norm_rows(xv):
        return _mod_norm(xv, g2, m[3:4], m[4:5])

    u_scr[0:tm] = mod_norm_rows(x1_scr[...]).astype(BF16)
    u_next = jnp.where(t == pl.num_programs(1) - 1, 0.0, mod_norm_rows(x1_next))
    u_prev = jnp.where(t == 0, 0.0, mod_norm_rows(x1_prev))
    u_scr[tm:rows] = jnp.concatenate([u_next, u_prev], axis=0).astype(BF16)

    def conv(h, col0, width):
        cw = cw_ref[:, col0:col0 + width]
        out = (pltpu.roll(h, 1, 0) * cw[0:1] + h * cw[1:2] + pltpu.roll(h, rows - 1, 0) * cw[2:3]
               + cb_ref[:, col0:col0 + width])
        return out[0:tm]

    u = u_scr[...]
    for c in range(d_ff // FFN_CHUNK):
        c0 = c * FFN_CHUNK
        gate = conv(jnp.dot(u, wup_ref[:, c0:c0 + FFN_CHUNK], preferred_element_type=F32), c0, FFN_CHUNK)
        val = conv(jnp.dot(u, wup_ref[:, d_ff + c0:d_ff + c0 + FFN_CHUNK], preferred_element_type=F32),
                   d_ff + c0, FFN_CHUNK)
        act_scr[:, c0:c0 + FFN_CHUNK] = (gate * jax.nn.sigmoid(gate) * val).astype(BF16)

    y2 = jnp.dot(act_scr[...], wdn_ref[...], preferred_element_type=F32)
    o_ref[0] = x1_scr[...] + m[5:6] * y2


def _mix_ffn(x, att, br, m, wa, wb, g2, w_up, conv_w, conv_b, w_down, row0, n_rows, tm, ctx_mod, out_rows,
             prev_out=None):
    B, Lt, D = x.shape
    d_ff = w_down.shape[0]
    assert row0 % tm == 0 and n_rows % tm == 0 and tm % BF16_ROWS == 0 and 2 * FFN_HALO == BF16_ROWS
    t0 = row0 // tm

    def tile(w):
        return pl.BlockSpec((1, tm, w), lambda b, t: (b, t0 + t, 0))

    def halo(w, h, prev):
        per, first, n = tm // h, row0 // h, Lt // h
        if prev:
            return pl.BlockSpec((1, h, w), lambda b, t: (b, jnp.maximum(first + t * per - 1, 0), 0))
        return pl.BlockSpec((1, h, w), lambda b, t: (b, jnp.minimum(first + (t + 1) * per, n - 1), 0))

    const = lambda a: pl.BlockSpec(a.shape, lambda b, t: (0, 0))
    once = lambda a: pl.BlockSpec(a.shape, lambda b, t: (0, 0), pipeline_mode=pl.Buffered(1))
    wa_w, wb_w = att.shape[2], br.shape[2]
    aliased = prev_out is not None
    in_specs = [
        tile(D), halo(D, FFN_HALO, True), halo(D, FFN_HALO, False),
        tile(wa_w), halo(wa_w, BF16_ROWS, True), halo(wa_w, BF16_ROWS, False),
        tile(wb_w), halo(wb_w, BF16_ROWS, True), halo(wb_w, BF16_ROWS, False),
        pl.BlockSpec((1, N_MOD, D), (lambda b, t: (B, 0, 0)) if ctx_mod else (lambda b, t: (b, 0, 0))),
        const(g2), once(wa), once(wb), once(w_up), const(conv_w), const(conv_b), once(w_down),
    ]
    args = [x, x, x, att, att, att, br, br, br, m, g2, wa, wb, w_up, conv_w, conv_b, w_down]
    if aliased:
        in_specs = [pl.BlockSpec(memory_space=pl.ANY)] + in_specs
        args = [prev_out] + args
    return pl.pallas_call(
        functools.partial(_mix_ffn_kernel, d_ff=d_ff, aliased=aliased),
        grid=(B, n_rows // tm),
        in_specs=in_specs,
        out_specs=pl.BlockSpec((1, tm, D), lambda b, t: (b, t0 + t, 0)),
        out_shape=jax.ShapeDtypeStruct((B, out_rows, D), F32),
        scratch_shapes=[
            pltpu.VMEM((tm + 2 * BF16_ROWS, wa_w), BF16), pltpu.VMEM((tm + 2 * BF16_ROWS, wb_w), BF16),
            pltpu.VMEM((tm, D), F32), pltpu.VMEM((tm + 2 * FFN_HALO, D), BF16), pltpu.VMEM((tm, d_ff), BF16)],
        input_output_aliases={0: 0} if aliased else {},
        compiler_params=_cparams(("parallel", "arbitrary")),
        name="mix_ffn",
    )(*args)


def _rope_tables_T(rows, rot_dim, n_ctx):
    n_freq = rot_dim // 4
    inv = ROPE_BASE ** (-jnp.arange(n_freq, dtype=F32) / n_freq)
    row = jnp.broadcast_to(jnp.arange(rows, dtype=F32)[:, None], (rows, GRID_W)).reshape(-1)
    col = jnp.broadcast_to(jnp.arange(GRID_W, dtype=F32)[None, :], (rows, GRID_W)).reshape(-1)
    ang = jnp.concatenate([row[:, None] * inv, col[:, None] * inv], axis=-1)
    cosT = jnp.concatenate([jnp.cos(ang).T, jnp.ones((rot_dim // 2, n_ctx), F32)], axis=1)
    sinT = jnp.concatenate([jnp.sin(ang).T, jnp.zeros((rot_dim // 2, n_ctx), F32)], axis=1)
    return cosT, sinT


def _dft_tables(n):
    idx = (jnp.arange(n, dtype=jnp.int32)[:, None] * jnp.arange(n, dtype=jnp.int32)[None, :]) % n
    ang = idx.astype(F32) * (2.0 * math.pi / n)
    return jnp.cos(ang), jnp.sin(ang)


def _dft_tables_split(n, r):
    l = jnp.arange(n, dtype=jnp.int32)[:, None]
    kk = jnp.arange(r, dtype=jnp.int32)[None, :]
    a = ((kk * l) % r).astype(F32) * (2.0 * math.pi / r)
    b = ((kk * l) % n).astype(F32) * (2.0 * math.pi / n)
    ca, sa, cb, sb = jnp.cos(a)[:, :, None], jnp.sin(a)[:, :, None], jnp.cos(b)[:, None, :], jnp.sin(b)[:, None, :]
    return (ca * cb - sa * sb).reshape(n, n), (sa * cb + ca * sb).reshape(n, n)


def _block_diag(w):
    g, a, b = w.shape
    eye = jnp.eye(g, dtype=w.dtype)
    return (eye[:, None, :, None] * w[:, :, None, :]).reshape(g * a, g * b)


def _col(v):
    return v.reshape(-1, 1).astype(F32)


def kernel(x, c, ctx, c_ctx, mod_w, mod_b, norm1_g, norm2_g, mla_w_in, mla_cq_g, mla_ckv_g, mla_w_uq, mla_w_ukv,
           mla_q_g, mla_k_g, fnet_w, even_w_out, win_w_in, win_q_g, win_k_g, win_sink, pool_w, pool_scale, odd_w_out,
           ffn_up, ffn_conv_w, ffn_conv_b, ffn_down):
    B, L, D = x.shape
    C = ctx.shape[1]
    depth = mod_w.shape[0]
    assert L % TOK_TILE == 0 and C == TOK_TILE and B % FNET_BGRP == 0
    n_lat_tiles = L // TOK_TILE
    rows = L // GRID_W

    pad = (-(B + 1)) % 8
    cc = jnp.concatenate([c, c_ctx[None, :], jnp.zeros((pad, D), F32)], axis=0)
    mods = _modulation(cc, mod_w, mod_b).reshape(depth, B + 1 + pad, N_MOD, D)

    cos_a, sin_a = _rope_tables_T(rows, A_ROPE, C)
    cos_w, sin_w = _rope_tables_T(rows, C_HDIM, C)
    r_lat = math.isqrt(L)
    c_lat, s_lat = (t.astype(BF16) for t in (_dft_tables_split(L, r_lat) if r_lat * r_lat == L else _dft_tables(L)))
    c_ctx_t, s_ctx_t = (t.astype(BF16) for t in _dft_tables(C))
    cc64, ss64 = _dft_tables(B_GDIM)
    eye_g = jnp.eye(B_GROUPS, dtype=F32)
    cs_bd = jnp.concatenate([jnp.kron(eye_g, cc64), -jnp.kron(eye_g, ss64)], axis=0).astype(BF16)

    nh = C_HEADS // (C_KV_HEADS // 2)
    head_order = [p * nh + e * (nh // 2) + j for p in range(C_KV_HEADS // 2) for j in range(nh // 2) for e in range(2)]
    att_perm = jnp.asarray([h * C_HDIM + d for h in head_order for d in range(C_HDIM)], jnp.int32)

    h = jnp.concatenate([x, ctx], axis=1)
    for i in range(depth):
        j = i // 2
        m = mods[i]
        g1 = norm1_g[i][None, :]
        if i % 2 == 0:
            q, kT, vT, z = _even_in(
                h, m, g1, mla_w_in[j].T.astype(BF16), _col(mla_cq_g[j]), mla_w_uq[j].T.astype(BF16),
                _col(mla_q_g[j]), _col(mla_ckv_g[j]), mla_w_ukv[j].T.astype(BF16), _col(mla_k_g[j]),
                cos_a, sin_a, n_lat_tiles)
            att = _mla_attn(q, kT, vT, 0, L, MLA_Q_TILE, 0, L + C)
            att = _mla_attn(q, kT, vT, L, C, C, L, C, prev_out=att)
            wf_bd = _block_diag(fnet_w[j]).astype(BF16)
            zr = jnp.roll(jnp.flip(z[:, :L], axis=1), 1, axis=1)
            br = _fnet_fold(z, zr, c_lat, s_lat, cs_bd, wf_bd, L + C)
            br = _fnet(z, c_ctx_t, s_ctx_t, cs_bd, wf_bd, L, L + C, prev_out=br)
            w_out = even_w_out[j].astype(BF16)
            wa, wb = w_out[:A_HEADS * A_V], w_out[A_HEADS * A_V:]
        else:
            q, k2, v2, pz = _odd_in(h, m, g1, win_w_in[j].T.astype(BF16), _col(win_q_g[j]), _col(win_k_g[j]),
                                    cos_w, sin_w, n_lat_tiles)
            att = _win_attn(win_sink[j].reshape(-1), q, k2, v2, L)
            br = _pool(pz, _block_diag(pool_w[j]).astype(BF16), pool_scale[j][None, :], L)
            w_out = odd_w_out[j].astype(BF16)
            wa, wb = w_out[:C_QW][att_perm], w_out[C_QW:]
        ffn_args = (att, br, m, wa, wb, norm2_g[i][None, :], ffn_up[i].astype(BF16), ffn_conv_w[i],
                    ffn_conv_b[i][None, :], ffn_down[i].astype(BF16))
        h_in = h
        if i < depth - 1:
            h = _mix_ffn(h_in, *ffn_args, 0, L, FFN_LAT_TILE, False, L + C)
            h = _mix_ffn(h_in, *ffn_args, L, C, C, True, L + C, prev_out=h)
        else:
            h = _mix_ffn(h_in, *ffn_args, 0, L, FFN_LAT_TILE, False, L)
    return h
```

```python
import functools
import math

import jax
import jax.numpy as jnp
from jax import lax
from jax.experimental import pallas as pl
from jax.experimental.pallas import tpu as pltpu

F32 = jnp.float32
BF16 = jnp.bfloat16

GRID_W = 64
N_MOD = 6
EPS = 1e-6
ROPE_BASE = 10000.0
NEG_INF = -1e30
A_HEADS = 12
A_NOPE = 64
A_ROPE = 32
A_QK = A_NOPE + A_ROPE
A_V = 64
A_Q_LORA = 256
A_KV_LORA = 128
A_IN = A_Q_LORA + A_KV_LORA + A_ROPE
B_GROUPS = 4
B_GDIM = 64
B_WIDTH = B_GROUPS * B_GDIM
C_HEADS = 12
C_KV_HEADS = 4
C_GROUP = C_HEADS // C_KV_HEADS
C_HDIM = 64
C_WINDOW = 128
C_QW = C_HEADS * C_HDIM
C_KW = C_KV_HEADS * C_HDIM
D_GROUPS = 4
D_GDIM = 64
D_WIDTH = D_GROUPS * D_GDIM
D_WINDOWS = (2, 4, 8, 16)

LANES = 128
BF16_ROWS = 16
HEAD_PAD = LANES
TOK_TILE = 256
FFN_HALO = 8
FFN_CHUNK = 256
FFN_ROW_BLOCK = 512
FFN_LAT_TILE = 1024
FNET_BGRP = 4
IN_TILE = 512
MLA_Q_TILE = 512
MLA_KEY_CHUNK = 256
MLA_PV_LAG = 1
LOG2_E = math.log2(math.e)
WIN_KEY_CHUNK = 256
VMEM_LIMIT = 56 * 1024 * 1024

NT_DIMS = (((1,), (1,)), ((), ()))


def _cparams(sem):
    return pltpu.CompilerParams(dimension_semantics=sem, vmem_limit_bytes=VMEM_LIMIT)


def _mod_norm(x, g, shift, scale):
    ms = jnp.mean(x * x, axis=-1, keepdims=True)
    y = x * lax.rsqrt(ms + EPS) * g
    return y * (1.0 + scale) + shift


def _rms_rows(xT, g_col):
    ms = jnp.mean(xT * xT, axis=0, keepdims=True)
    return xT * lax.rsqrt(ms + EPS) * g_col


def _rope_rows(x1, x2, cos, sin):
    return x1 * cos - x2 * sin, x1 * sin + x2 * cos


def _mod_kernel(cc_ref, w_ref, b_ref, o_ref):
    cc = cc_ref[...]
    a = (cc * jax.nn.sigmoid(cc)).astype(BF16)
    w = w_ref[0].astype(BF16)
    o_ref[0] = jnp.dot(a, w, preferred_element_type=F32) + b_ref[0]


def _modulation(cc, mod_w, mod_b):
    depth, d, n = mod_w.shape
    rows = cc.shape[0]
    tn = 1024
    return pl.pallas_call(
        _mod_kernel,
        grid=(depth, n // tn),
        in_specs=[
            pl.BlockSpec((rows, d), lambda i, j: (0, 0)),
            pl.BlockSpec((1, d, tn), lambda i, j: (i, 0, j)),
            pl.BlockSpec((1, 1, tn), lambda i, j: (i, 0, j)),
        ],
        out_specs=pl.BlockSpec((1, rows, tn), lambda i, j: (i, 0, j)),
        out_shape=jax.ShapeDtypeStruct((depth, rows, n), F32),
        compiler_params=_cparams(("arbitrary", "arbitrary")),
        name="modulation",
    )(cc, mod_w, mod_b.reshape(depth, 1, n))


def _even_in_kernel(x_ref, xc_ref, m_ref, g1_ref, w_in_ref, cqg_ref, wuq_ref, qg_ref, ckvg_ref, wukv_ref, kg_ref,
                    cos_ref, sin_ref, q_ref, k_ref, v_ref, z_ref, *, n_lat_tiles):
    tm = x_ref.shape[1]
    m = m_ref[0]
    xc = jnp.concatenate([xc_ref[0]] * (tm // xc_ref.shape[1]), axis=0)
    x = jnp.where(pl.program_id(1) < n_lat_tiles, x_ref[0], xc)
    u = _mod_norm(x, g1_ref[...], m[0:1], m[1:2]).astype(BF16)
    hT = lax.dot_general(w_in_ref[...], u, NT_DIMS, preferred_element_type=F32)
    cos = cos_ref[...]
    sin = sin_ref[...]
    scale = A_QK ** -0.5 * LOG2_E
    half = A_ROPE // 2
    zpad = jnp.zeros((HEAD_PAD - A_QK, tm), F32)
    ones_rows = jnp.ones((BF16_ROWS, tm), F32)

    z_ref[0] = hT[A_IN:A_IN + B_WIDTH].T.astype(BF16)

    cqn = _rms_rows(hT[0:A_Q_LORA], cqg_ref[...]).astype(BF16)
    qT = jnp.dot(wuq_ref[...], cqn, preferred_element_type=F32)
    qg = qg_ref[...] * scale
    for h in range(A_HEADS):
        qn = _rms_rows(qT[h * A_QK:(h + 1) * A_QK], qg)
        o1, o2 = _rope_rows(qn[A_NOPE:A_NOPE + half], qn[A_NOPE + half:A_QK], cos, sin)
        full = jnp.concatenate([qn[0:A_NOPE], o1, o2, zpad], axis=0)
        q_ref[0, h] = full.astype(BF16)

    ckvn = _rms_rows(hT[A_Q_LORA:A_Q_LORA + A_KV_LORA], ckvg_ref[...]).astype(BF16)
    kvT = jnp.dot(wukv_ref[...], ckvn, preferred_element_type=F32)
    kr = hT[A_Q_LORA + A_KV_LORA:A_IN]
    kr_ss = jnp.sum(kr * kr, axis=0, keepdims=True)
    kg = kg_ref[...]
    hw = A_NOPE + A_V
    for h in range(A_HEADS):
        kn = kvT[h * hw:h * hw + A_NOPE]
        r = lax.rsqrt((jnp.sum(kn * kn, axis=0, keepdims=True) + kr_ss) * (1.0 / A_QK) + EPS)
        knn = kn * r * kg[0:A_NOPE]
        krn = kr * r * kg[A_NOPE:A_QK]
        o1, o2 = _rope_rows(krn[0:half], krn[half:A_ROPE], cos, sin)
        k_ref[0, h] = jnp.concatenate([knn, o1, o2, zpad], axis=0).T.astype(BF16)
        vh = kvT[h * hw + A_NOPE:(h + 1) * hw]
        v_ref[0, h] = jnp.concatenate([vh, ones_rows], axis=0).astype(BF16)


def _even_in(x, xc, xc_row0, n_ctx, m, g1, w_inT, cq_g, w_uqT, q_g, ckv_g, w_ukvT, k_g, cosT, sinT, n_lat):
    B, _, D = x.shape
    Lt = n_lat + n_ctx
    tm = IN_TILE
    assert n_lat % tm == 0 and tm % n_ctx == 0 and xc_row0 % n_ctx == 0
    nt = pl.cdiv(Lt, tm)
    n_lat_tiles = n_lat // tm
    full2 = lambda a: pl.BlockSpec(a.shape, lambda b, t: (0, 0))
    return pl.pallas_call(
        functools.partial(_even_in_kernel, n_lat_tiles=n_lat_tiles),
        grid=(B, nt),
        in_specs=[
            pl.BlockSpec((1, tm, D), lambda b, t: (b, jnp.minimum(t, n_lat_tiles - 1), 0)),
            pl.BlockSpec((1, n_ctx, D), lambda b, t: (b, xc_row0 // n_ctx, 0)),
            pl.BlockSpec((1, N_MOD, D), lambda b, t: (jnp.where(t < n_lat_tiles, b, B), 0, 0)),
            full2(g1), full2(w_inT), full2(cq_g), full2(w_uqT), full2(q_g), full2(ckv_g), full2(w_ukvT), full2(k_g),
            pl.BlockSpec((A_ROPE // 2, tm), lambda b, t: (0, t)),
            pl.BlockSpec((A_ROPE // 2, tm), lambda b, t: (0, t)),
        ],
        out_specs=[
            pl.BlockSpec((1, A_HEADS, HEAD_PAD, tm), lambda b, t: (b, 0, 0, t)),
            pl.BlockSpec((1, A_HEADS, tm, HEAD_PAD), lambda b, t: (b, 0, t, 0)),
            pl.BlockSpec((1, A_HEADS, A_V + BF16_ROWS, tm), lambda b, t: (b, 0, 0, t)),
            pl.BlockSpec((1, tm, B_WIDTH), lambda b, t: (b, t, 0)),
        ],
        out_shape=[
            jax.ShapeDtypeStruct((B, A_HEADS, HEAD_PAD, Lt), BF16),
            jax.ShapeDtypeStruct((B, A_HEADS, Lt, HEAD_PAD), BF16),
            jax.ShapeDtypeStruct((B, A_HEADS, A_V + BF16_ROWS, Lt), BF16),
            jax.ShapeDtypeStruct((B, Lt, B_WIDTH), BF16),
        ],
        compiler_params=_cparams(("parallel", "arbitrary")),
        name="even_in",
    )(x, xc, m, g1, w_inT, cq_g, w_uqT, q_g, ckv_g, w_ukvT, k_g, cosT, sinT)


def _mla_attn_kernel(*refs, aliased):
    q_ref, k_ref, v_ref, o_ref = refs[1:] if aliased else refs
    nk = k_ref.shape[2]
    qs = [q_ref[0, j] for j in range(2)]
    m = [None, None]
    acc = [None, None]
    pend = [[], []]
    chunks = [(k0, min(k0 + MLA_KEY_CHUNK, nk)) for k0 in range(0, nk, MLA_KEY_CHUNK)]

    def scores(j, k0, k1):
        s = jnp.dot(k_ref[0, j, k0:k1, :], qs[j], preferred_element_type=F32)
        mx = jnp.max(s, axis=0, keepdims=True)
        m_new = mx if m[j] is None else jnp.maximum(m[j], mx)
        alpha = None if m[j] is None else jnp.exp2(m[j] - m_new)
        pend[j].append((jnp.exp2(s - m_new).astype(BF16), alpha, k0, k1))
        m[j] = m_new

    def weighted_values(j):
        p, alpha, k0, k1 = pend[j].pop(0)
        pv = jnp.dot(v_ref[0, j, :, k0:k1], p, preferred_element_type=F32)
        acc[j] = pv if alpha is None else alpha * acc[j] + pv

    for i in range(len(chunks) + MLA_PV_LAG):
        for j in range(2):
            if i < len(chunks):
                scores(j, *chunks[i])
            if i >= MLA_PV_LAG:
                weighted_values(j)
    outs = [(a / a[A_V:A_V + 1])[0:A_V] for a in acc]
    o_ref[0] = jnp.concatenate(outs, axis=0).T.astype(o_ref.dtype)


def _mla_attn(q, kT, vT, q_row0, n_q, tq, k_col0, n_k, prev_out=None):
    B, H, _, Lt = q.shape
    assert q_row0 % tq == 0 and n_q % tq == 0 and k_col0 % n_k == 0
    qb, kb = q_row0 // tq, k_col0 // n_k
    aliased = prev_out is not None
    in_specs = [
        pl.BlockSpec((1, 2, HEAD_PAD, tq), lambda b, h, t: (b, h, 0, qb + t)),
        pl.BlockSpec((1, 2, n_k, HEAD_PAD), lambda b, h, t: (b, h, kb, 0)),
        pl.BlockSpec((1, 2, A_V + BF16_ROWS, n_k), lambda b, h, t: (b, h, 0, kb)),
    ]
    args = [q, kT, vT]
    if aliased:
        in_specs = [pl.BlockSpec(memory_space=pl.ANY)] + in_specs
        args = [prev_out] + args
    return pl.pallas_call(
        functools.partial(_mla_attn_kernel, aliased=aliased),
        grid=(B, H // 2, n_q // tq),
        in_specs=in_specs,
        out_specs=pl.BlockSpec((1, tq, 2 * A_V), lambda b, h, t: (b, qb + t, h)),
        out_shape=jax.ShapeDtypeStruct((B, Lt, H * A_V), BF16),
        input_output_aliases={0: 0} if aliased else {},
        compiler_params=_cparams(("parallel", "arbitrary", "arbitrary")),
        name="mla_attn",
    )(*args)


def _fnet_kernel(*refs, norm, aliased):
    if aliased:
        refs = refs[1:]
    c_ref, s_ref, z_ref, cs_ref, wf_ref, o_ref, p_acc, q_acc = refs
    k = pl.program_id(2)

    @pl.when(k == 0)
    def _():
        p_acc[...] = jnp.zeros_like(p_acc)
        q_acc[...] = jnp.zeros_like(q_acc)

    c = c_ref[...]
    s = s_ref[...]
    for j in range(z_ref.shape[0]):
        z = z_ref[j]
        p_acc[j] += jnp.dot(c, z, preferred_element_type=F32)
        q_acc[j] += jnp.dot(s, z, preferred_element_type=F32)

    @pl.when(k == pl.num_programs(2) - 1)
    def _():
        for j in range(z_ref.shape[0]):
            pq = jnp.concatenate([p_acc[j], q_acc[j]], axis=1).astype(BF16)
            y = jnp.dot(pq, cs_ref[...], preferred_element_type=F32) * norm
            o_ref[j] = jnp.dot(y.astype(BF16), wf_ref[...], preferred_element_type=F32).astype(o_ref.dtype)


def _fnet(z, c_tab, s_tab, cs_bd, wf_bd, row0, out_rows, prev_out=None):
    B = z.shape[0]
    n = c_tab.shape[0]
    tm = min(n, 1024)
    tk = min(n, 1024)
    g = FNET_BGRP
    norm = 1.0 / math.sqrt(n * B_GDIM)
    aliased = prev_out is not None
    in_specs = [
        pl.BlockSpec((tm, tk), lambda b, i, k: (i, k)),
        pl.BlockSpec((tm, tk), lambda b, i, k: (i, k)),
        pl.BlockSpec((g, tk, B_WIDTH), lambda b, i, k: (b, row0 // tk + k, 0)),
        pl.BlockSpec(cs_bd.shape, lambda b, i, k: (0, 0)),
        pl.BlockSpec(wf_bd.shape, lambda b, i, k: (0, 0)),
    ]
    args = [c_tab, s_tab, z, cs_bd, wf_bd]
    if aliased:
        in_specs = [pl.BlockSpec(memory_space=pl.ANY)] + in_specs
        args = [prev_out] + args
    return pl.pallas_call(
        functools.partial(_fnet_kernel, norm=norm, aliased=aliased),
        grid=(B // g, n // tm, n // tk),
        in_specs=in_specs,
        out_specs=pl.BlockSpec((g, tm, B_WIDTH), lambda b, i, k: (b, row0 // tm + i, 0)),
        out_shape=jax.ShapeDtypeStruct((B, out_rows, B_WIDTH), BF16),
        scratch_shapes=[pltpu.VMEM((g, tm, B_WIDTH), F32), pltpu.VMEM((g, tm, B_WIDTH), F32)],
        input_output_aliases={0: 0} if aliased else {},
        compiler_params=_cparams(("parallel", "arbitrary", "arbitrary")),
        name="fnet",
    )(*args)


def _odd_in_kernel(x_ref, m_ref, g1_ref, w_in_ref, qg_ref, kg_ref, cos_ref, sin_ref, q_ref, k_ref, v_ref, pz_ref):
    tm = x_ref.shape[1]
    m = m_ref[0]
    u = _mod_norm(x_ref[0], g1_ref[...], m[0:1], m[1:2]).astype(BF16)
    hT = lax.dot_general(w_in_ref[...], u, NT_DIMS, preferred_element_type=F32)
    cos = cos_ref[...]
    sin = sin_ref[...]
    half = C_HDIM // 2
    zeros = jnp.zeros((C_HDIM, tm), F32)

    def norm_rope(xT, g_col):
        xn = _rms_rows(xT, g_col)
        o1, o2 = _rope_rows(xn[0:half], xn[half:C_HDIM], cos, sin)
        return jnp.concatenate([o1, o2], axis=0)

    qg = qg_ref[...] * (C_HDIM ** -0.5 * LOG2_E)
    for h in range(C_HEADS):
        qh = norm_rope(hT[h * C_HDIM:(h + 1) * C_HDIM], qg)
        parts = [qh, zeros] if (h // C_GROUP) % 2 == 0 else [zeros, qh]
        q_ref[0, h] = jnp.concatenate(parts, axis=0).astype(BF16)

    kg = kg_ref[...]
    for p in range(C_KV_HEADS // 2):
        ks = [norm_rope(hT[C_QW + (2 * p + e) * C_HDIM:C_QW + (2 * p + e + 1) * C_HDIM], kg) for e in range(2)]
        k_ref[0, p] = jnp.concatenate(ks, axis=0).T.astype(BF16)
        v0 = C_QW + C_KW + 2 * p * C_HDIM
        v_ref[0, p] = hT[v0:v0 + 2 * C_HDIM].T.astype(BF16)

    pz_ref[0] = hT[C_QW + 2 * C_KW:C_QW + 2 * C_KW + D_WIDTH].T


def _odd_in(x, m, g1, w_inT, q_g, k_g, cosT, sinT, n_lat_tiles):
    B, Lt, D = x.shape
    tm = TOK_TILE
    nt = Lt // tm
    full2 = lambda a: pl.BlockSpec(a.shape, lambda b, t: (0, 0))
    np_ = C_KV_HEADS // 2
    return pl.pallas_call(
        _odd_in_kernel,
        grid=(B, nt),
        in_specs=[
            pl.BlockSpec((1, tm, D), lambda b, t: (b, t, 0)),
            pl.BlockSpec((1, N_MOD, D), lambda b, t: (jnp.where(t < n_lat_tiles, b, B), 0, 0)),
            full2(g1), full2(w_inT), full2(q_g), full2(k_g),
            pl.BlockSpec((C_HDIM // 2, tm), lambda b, t: (0, t)),
            pl.BlockSpec((C_HDIM // 2, tm), lambda b, t: (0, t)),
        ],
        out_specs=[
            pl.BlockSpec((1, C_HEADS, HEAD_PAD, tm), lambda b, t: (b, 0, 0, t)),
            pl.BlockSpec((1, np_, tm, HEAD_PAD), lambda b, t: (b, 0, t, 0)),
            pl.BlockSpec((1, np_, tm, HEAD_PAD), lambda b, t: (b, 0, t, 0)),
            pl.BlockSpec((1, tm, D_WIDTH), lambda b, t: (b, t, 0)),
        ],
        out_shape=[
            jax.ShapeDtypeStruct((B, C_HEADS, HEAD_PAD, Lt), BF16),
            jax.ShapeDtypeStruct((B, np_, Lt, HEAD_PAD), BF16),
            jax.ShapeDtypeStruct((B, np_, Lt, HEAD_PAD), BF16),
            jax.ShapeDtypeStruct((B, Lt, D_WIDTH), F32),
        ],
        compiler_params=_cparams(("parallel", "arbitrary")),
        name="odd_in",
    )(x, m, g1, w_inT, q_g, k_g, cosT, sinT)


def _win_attn_kernel(sink_ref, q_ref, k_ref, v_ref, o_ref, *, n_lat_tiles, n_lat):
    pair = pl.program_id(1)
    t = pl.program_id(2)
    nh = q_ref.shape[1]
    tq = q_ref.shape[3]
    half = nh // 2
    ck = WIN_KEY_CHUNK
    col = lax.broadcasted_iota(jnp.int32, (1, 2 * tq), 1)
    row = lax.broadcasted_iota(jnp.int32, (HEAD_PAD, tq), 0)

    def run(chunks):
        qg = [jnp.concatenate([q_ref[0, j], q_ref[0, half + j]], axis=1) for j in range(half)]
        sk = [jnp.where(col < tq, sink_ref[pair * nh + j], sink_ref[pair * nh + half + j]) * LOG2_E
              for j in range(half)]
        m = list(sk)
        acc = [None] * half
        pend = [[] for _ in range(half)]

        def scores(j, kb, valid):
            s = jnp.dot(kb, qg[j], preferred_element_type=F32)
            if valid is not None:
                s = jnp.where(valid, s, NEG_INF)
            m_new = jnp.maximum(m[j], jnp.max(s, axis=0, keepdims=True))
            pend[j].append((jnp.exp2(s - m_new).astype(BF16), jnp.exp2(m[j] - m_new)))
            m[j] = m_new

        def weighted_values(j, vbT):
            p, alpha = pend[j].pop(0)
            pv = jnp.dot(vbT, p, preferred_element_type=F32)
            acc[j] = pv if acc[j] is None else alpha * acc[j] + pv

        for i in range(len(chunks) + 1):
            for j in range(half):
                if i < len(chunks):
                    scores(j, chunks[i][0], chunks[i][2])
                if i >= 1:
                    weighted_values(j, chunks[i - 1][1])
        for j in range(half):
            l = acc[j][HEAD_PAD:HEAD_PAD + 1] + jnp.exp2(sk[j] - m[j])
            o = acc[j][0:HEAD_PAD] / l
            blk = jnp.where(row < C_HDIM, o[:, 0:tq], o[:, tq:2 * tq])
            o_ref[0, :, j * HEAD_PAD:(j + 1) * HEAD_PAD] = blk.T.astype(o_ref.dtype)

    def kv_chunk(start):
        kb = k_ref[0, 0, pl.ds(start, ck), :]
        vT = v_ref[0, 0, pl.ds(start, ck), :].astype(F32).T
        return kb, jnp.concatenate([vT, jnp.ones((BF16_ROWS, ck), F32)], axis=0).astype(BF16)

    @pl.when(t < n_lat_tiles)
    def _():
        start = pl.multiple_of(jnp.maximum(t * tq - C_WINDOW, 0), C_WINDOW)
        qpos = t * tq + (lax.broadcasted_iota(jnp.int32, (ck, 2 * tq), 1) & (tq - 1))
        chunks = []
        for c in range((tq + 2 * C_WINDOW) // ck):
            kpos = start + c * ck + lax.broadcasted_iota(jnp.int32, (ck, 2 * tq), 0)
            valid = (jnp.abs(kpos - qpos) <= C_WINDOW) & (kpos < n_lat)
            chunks.append(kv_chunk(start + c * ck) + (valid,))
        chunks.append(kv_chunk(n_lat) + (None,))
        run(chunks)

    @pl.when(t >= n_lat_tiles)
    def _():
        run([kv_chunk(n_lat) + (None,)])


def _win_attn(sink, q, k2, v2, n_lat, ctx_queries):
    B, H, _, Lt = q.shape
    tq = TOK_TILE
    npair = k2.shape[1]
    nh = H // npair
    return pl.pallas_call(
        functools.partial(_win_attn_kernel, n_lat_tiles=n_lat // tq, n_lat=n_lat),
        grid=(B, npair, (Lt if ctx_queries else n_lat) // tq),
        in_specs=[
            pl.BlockSpec(memory_space=pltpu.SMEM),
            pl.BlockSpec((1, nh, HEAD_PAD, tq), lambda b, p, t: (b, p, 0, t)),
            pl.BlockSpec((1, 1, Lt, HEAD_PAD), lambda b, p, t: (b, p, 0, 0)),
            pl.BlockSpec((1, 1, Lt, HEAD_PAD), lambda b, p, t: (b, p, 0, 0)),
        ],
        out_specs=pl.BlockSpec((1, tq, nh * C_HDIM), lambda b, p, t: (b, t, p)),
        out_shape=jax.ShapeDtypeStruct((B, Lt, H * C_HDIM), BF16),
        compiler_params=_cparams(("parallel", "arbitrary", "arbitrary")),
        name="win_attn",
    )(sink, q, k2, v2)


def _pool_lane_tile(z, wp, ps, g0):
    n = z.shape[0]
    row = lax.broadcasted_iota(jnp.int32, z.shape, 0)
    low = lax.broadcasted_iota(jnp.int32, z.shape, 1) < D_GDIM

    def shift_down(a, s):
        return jnp.where(row >= s, pltpu.roll(a, s, 0), 0.0)

    def shift_up(a, s):
        return jnp.where(row < n - s, pltpu.roll(a, n - s, 0), 0.0)

    h_lo, h_hi = D_WINDOWS[g0] // 2, D_WINDOWS[g0 + 1] // 2
    back, fwd, h = z, z, 1
    levels = {1: (z, z)}
    while h < h_hi:
        back = back + shift_down(back, h)
        fwd = fwd + shift_up(fwd, h)
        h *= 2
        levels[h] = (back, fwd)
    back_sel = jnp.where(low, levels[h_lo][0], levels[h_hi][0])
    fwd_sel = jnp.where(low, levels[h_lo][1], levels[h_hi][1])
    win_sum = shift_down(back_sel, 1) + fwd_sel
    half = jnp.where(low, h_lo, h_hi)
    cnt = jnp.minimum(row + half, n) - jnp.maximum(row - half, 0)
    pooled = (win_sum / cnt.astype(F32) - z).astype(BF16)
    return jnp.dot(pooled, wp, preferred_element_type=F32) * ps


def _pool_kernel(pz_ref, wp_ref, ps_ref, o_ref, *, n_lat):
    lt = pz_ref.shape[1]
    for lo, hi in ((0, n_lat), (n_lat, lt)):
        for c in range(D_GROUPS // 2):
            c0, c1 = c * LANES, (c + 1) * LANES
            y = _pool_lane_tile(pz_ref[0, lo:hi, c0:c1], wp_ref[c0:c1, c0:c1], ps_ref[:, c0:c1], 2 * c)
            o_ref[0, lo:hi, c0:c1] = y.astype(o_ref.dtype)


def _pool(pz, wp_bd, p_scale, n_lat):
    B, Lt, W = pz.shape
    return pl.pallas_call(
        functools.partial(_pool_kernel, n_lat=n_lat),
        grid=(B,),
        in_specs=[
            pl.BlockSpec((1, Lt, W), lambda b: (b, 0, 0)),
            pl.BlockSpec(wp_bd.shape, lambda b: (0, 0)),
            pl.BlockSpec(p_scale.shape, lambda b: (0, 0)),
        ],
        out_specs=pl.BlockSpec((1, Lt, W), lambda b: (b, 0, 0)),
        out_shape=jax.ShapeDtypeStruct((B, Lt, W), BF16),
        compiler_params=_cparams(("parallel",)),
        name="pool",
    )(pz, wp_bd, p_scale)


def _mix_ffn_kernel(*refs, d_ff, aliased):
    if aliased:
        refs = refs[1:]
    (x_ref, xp_ref, xn_ref, a_ref, ap_ref, an_ref, b_ref, bp_ref, bn_ref, m_ref, g2_ref, wa_ref, wb_ref,
     wup_ref, cw_ref, cb_ref, wdn_ref, o_ref, a_scr, b_scr, x1_scr, u_scr, act_scr) = refs
    t = pl.program_id(1)
    tm = x_ref.shape[1]
    hx = FFN_HALO
    hm = BF16_ROWS
    rows = tm + 2 * hx
    m = m_ref[0]
    g2 = g2_ref[...]

    for src, prv, nxt, scr in ((a_ref, ap_ref, an_ref, a_scr), (b_ref, bp_ref, bn_ref, b_scr)):
        scr[0:tm] = src[0]
        scr[tm:tm + hm] = nxt[0]
        scr[tm + hm:tm + 2 * hm] = prv[0]
    gate1 = m[2:3]

    def mod_norm_rows(xv):
        return _mod_norm(xv, g2, m[3:4], m[4:5])

    def out_proj(r0, r1):
        return (jnp.dot(a_scr[r0:r1], wa_ref[...], preferred_element_type=F32)
                + jnp.dot(b_scr[r0:r1], wb_ref[...], preferred_element_type=F32))

    nblk = max(tm // FFN_ROW_BLOCK, 1)
    rb = tm // nblk
    for i in range(nblk):
        r0, r1 = i * rb, (i + 1) * rb
        x1 = x_ref[0, r0:r1, :] + gate1 * out_proj(r0, r1)
        x1_scr[r0:r1] = x1
        u_scr[r0:r1] = mod_norm_rows(x1).astype(BF16)
    yh = out_proj(tm, tm + 2 * hm)
    x1_next = xn_ref[0] + gate1 * yh[0:hx]
    x1_prev = xp_ref[0] + gate1 * yh[2 * hm - hx:2 * hm]
    u_next = jnp.where(t == pl.num_programs(1) - 1, 0.0, mod_norm_rows(x1_next))
    u_prev = jnp.where(t == 0, 0.0, mod_norm_rows(x1_prev))
    u_scr[tm:rows] = jnp.concatenate([u_next, u_prev], axis=0).astype(BF16)

    def conv(h, col0, width):
        cw = cw_ref[:, col0:col0 + width]
        out = (pltpu.roll(h, 1, 0) * cw[0:1] + h * cw[1:2] + pltpu.roll(h, rows - 1, 0) * cw[2:3]
               + cb_ref[:, col0:col0 + width])
        return out[0:tm]

    u = u_scr[...]
    for c in range(d_ff // FFN_CHUNK):
        c0 = c * FFN_CHUNK
        gate = conv(jnp.dot(u, wup_ref[:, c0:c0 + FFN_CHUNK], preferred_element_type=F32), c0, FFN_CHUNK)
        val = conv(jnp.dot(u, wup_ref[:, d_ff + c0:d_ff + c0 + FFN_CHUNK], preferred_element_type=F32),
                   d_ff + c0, FFN_CHUNK)
        act_scr[:, c0:c0 + FFN_CHUNK] = (gate * jax.nn.sigmoid(gate) * val).astype(BF16)

    for i in range(nblk):
        r0, r1 = i * rb, (i + 1) * rb
        y2 = jnp.dot(act_scr[r0:r1], wdn_ref[...], preferred_element_type=F32)
        o_ref[0, r0:r1, :] = x1_scr[r0:r1] + m[5:6] * y2


def _mix_ffn(x, x_row0, att, br, m, wa, wb, g2, w_up, conv_w, conv_b, w_down, row0, n_rows, tm, ctx_mod, out_rows,
             prev_out=None):
    B, x_rows, D = x.shape
    Lt = att.shape[1]
    d_ff = w_down.shape[0]
    assert row0 % tm == 0 and x_row0 % tm == 0 and n_rows % tm == 0 and tm % BF16_ROWS == 0
    assert 2 * FFN_HALO == BF16_ROWS

    def tile(w, r0=row0):
        return pl.BlockSpec((1, tm, w), lambda b, t: (b, r0 // tm + t, 0))

    def halo(w, h, prev, r0=row0, total=Lt):
        per, first, n = tm // h, r0 // h, total // h
        if prev:
            return pl.BlockSpec((1, h, w), lambda b, t: (b, jnp.maximum(first + t * per - 1, 0), 0))
        return pl.BlockSpec((1, h, w), lambda b, t: (b, jnp.minimum(first + (t + 1) * per, n - 1), 0))

    const = lambda a: pl.BlockSpec(a.shape, lambda b, t: (0, 0))
    once = lambda a: pl.BlockSpec(a.shape, lambda b, t: (0, 0), pipeline_mode=pl.Buffered(1))
    wa_w, wb_w = att.shape[2], br.shape[2]
    aliased = prev_out is not None
    in_specs = [
        tile(D, x_row0), halo(D, FFN_HALO, True, x_row0, x_rows), halo(D, FFN_HALO, False, x_row0, x_rows),
        tile(wa_w), halo(wa_w, BF16_ROWS, True), halo(wa_w, BF16_ROWS, False),
        tile(wb_w), halo(wb_w, BF16_ROWS, True), halo(wb_w, BF16_ROWS, False),
        pl.BlockSpec((1, N_MOD, D), (lambda b, t: (B, 0, 0)) if ctx_mod else (lambda b, t: (b, 0, 0))),
        const(g2), once(wa), once(wb), once(w_up), const(conv_w), const(conv_b), once(w_down),
    ]
    args = [x, x, x, att, att, att, br, br, br, m, g2, wa, wb, w_up, conv_w, conv_b, w_down]
    if aliased:
        in_specs = [pl.BlockSpec(memory_space=pl.ANY)] + in_specs
        args = [prev_out] + args
    return pl.pallas_call(
        functools.partial(_mix_ffn_kernel, d_ff=d_ff, aliased=aliased),
        grid=(B, n_rows // tm),
        in_specs=in_specs,
        out_specs=tile(D),
        out_shape=jax.ShapeDtypeStruct((B, out_rows, D), F32),
        scratch_shapes=[
            pltpu.VMEM((tm + 2 * BF16_ROWS, wa_w), BF16), pltpu.VMEM((tm + 2 * BF16_ROWS, wb_w), BF16),
            pltpu.VMEM((tm, D), F32), pltpu.VMEM((tm + 2 * FFN_HALO, D), BF16), pltpu.VMEM((tm, d_ff), BF16)],
        input_output_aliases={0: 0} if aliased else {},
        compiler_params=_cparams(("parallel", "arbitrary")),
        name="mix_ffn",
    )(*args)


def _rope_tables_T(rows, rot_dim, n_ctx):
    n_freq = rot_dim // 4
    inv = ROPE_BASE ** (-jnp.arange(n_freq, dtype=F32) / n_freq)
    row = jnp.broadcast_to(jnp.arange(rows, dtype=F32)[:, None], (rows, GRID_W)).reshape(-1)
    col = jnp.broadcast_to(jnp.arange(GRID_W, dtype=F32)[None, :], (rows, GRID_W)).reshape(-1)
    ang = jnp.concatenate([row[:, None] * inv, col[:, None] * inv], axis=-1)
    cosT = jnp.concatenate([jnp.cos(ang).T, jnp.ones((rot_dim // 2, n_ctx), F32)], axis=1)
    sinT = jnp.concatenate([jnp.sin(ang).T, jnp.zeros((rot_dim // 2, n_ctx), F32)], axis=1)
    return cosT, sinT


def _dft_tables(n):
    idx = (jnp.arange(n, dtype=jnp.int32)[:, None] * jnp.arange(n, dtype=jnp.int32)[None, :]) % n
    ang = idx.astype(F32) * (2.0 * math.pi / n)
    return jnp.cos(ang), jnp.sin(ang)


def _dft_tables_split(n, r):
    l = jnp.arange(n, dtype=jnp.int32)[:, None]
    kk = jnp.arange(r, dtype=jnp.int32)[None, :]
    a = ((kk * l) % r).astype(F32) * (2.0 * math.pi / r)
    b = ((kk * l) % n).astype(F32) * (2.0 * math.pi / n)
    ca, sa, cb, sb = jnp.cos(a)[:, :, None], jnp.sin(a)[:, :, None], jnp.cos(b)[:, None, :], jnp.sin(b)[:, None, :]
    return (ca * cb - sa * sb).reshape(n, n), (sa * cb + ca * sb).reshape(n, n)


def _block_diag(w):
    g, a, b = w.shape
    eye = jnp.eye(g, dtype=w.dtype)
    return (eye[:, None, :, None] * w[:, :, None, :]).reshape(g * a, g * b)


def _col(v):
    return v.reshape(-1, 1).astype(F32)


def kernel(x, c, ctx, c_ctx, mod_w, mod_b, norm1_g, norm2_g, mla_w_in, mla_cq_g, mla_ckv_g, mla_w_uq, mla_w_ukv,
           mla_q_g, mla_k_g, fnet_w, even_w_out, win_w_in, win_q_g, win_k_g, win_sink, pool_w, pool_scale, odd_w_out,
           ffn_up, ffn_conv_w, ffn_conv_b, ffn_down):
    B, L, D = x.shape
    C = ctx.shape[1]
    depth = mod_w.shape[0]
    assert L % TOK_TILE == 0 and C == TOK_TILE and B % FNET_BGRP == 0
    n_lat_tiles = L // TOK_TILE
    rows = L // GRID_W

    pad = (-(B + 1)) % 8
    cc = jnp.concatenate([c, c_ctx[None, :], jnp.zeros((pad, D), F32)], axis=0)
    mods = _modulation(cc, mod_w, mod_b).reshape(depth, B + 1 + pad, N_MOD, D)

    cos_a, sin_a = _rope_tables_T(rows, A_ROPE, C)
    cos_w, sin_w = _rope_tables_T(rows, C_HDIM, C)
    r_lat = math.isqrt(L)
    c_lat, s_lat = (t.astype(BF16) for t in (_dft_tables_split(L, r_lat) if r_lat * r_lat == L else _dft_tables(L)))
    c_ctx_t, s_ctx_t = (t.astype(BF16) for t in _dft_tables(C))
    cc64, ss64 = _dft_tables(B_GDIM)
    eye_g = jnp.eye(B_GROUPS, dtype=F32)
    cs_bd = jnp.concatenate([jnp.kron(eye_g, cc64), -jnp.kron(eye_g, ss64)], axis=0).astype(BF16)

    nh = C_HEADS // (C_KV_HEADS // 2)
    head_order = [p * nh + e * (nh // 2) + j for p in range(C_KV_HEADS // 2) for j in range(nh // 2) for e in range(2)]
    att_perm = jnp.asarray([h * C_HDIM + d for h in head_order for d in range(C_HDIM)], jnp.int32)

    h = None
    for i in range(depth):
        j = i // 2
        m = mods[i]
        g1 = norm1_g[i][None, :]
        src = (x, ctx, 0) if h is None else (h, h, L)
        if i % 2 == 0:
            q, kT, vT, z = _even_in(
                *src, C, m, g1, mla_w_in[j].T.astype(BF16), _col(mla_cq_g[j]), mla_w_uq[j].T.astype(BF16),
                _col(mla_q_g[j]), _col(mla_ckv_g[j]), mla_w_ukv[j].T.astype(BF16), _col(mla_k_g[j]),
                cos_a, sin_a, L)
            att = _mla_attn(q, kT, vT, 0, L, MLA_Q_TILE, 0, L + C)
            att = _mla_attn(q, kT, vT, L, C, C, L, C, prev_out=att)
            wf_bd = _block_diag(fnet_w[j]).astype(BF16)
            br = _fnet(z, c_lat, s_lat, cs_bd, wf_bd, 0, L + C)
            br = _fnet(z, c_ctx_t, s_ctx_t, cs_bd, wf_bd, L, L + C, prev_out=br)
            w_out = even_w_out[j].astype(BF16)
            wa, wb = w_out[:A_HEADS * A_V], w_out[A_HEADS * A_V:]
        else:
            q, k2, v2, pz = _odd_in(h, m, g1, win_w_in[j].T.astype(BF16), _col(win_q_g[j]), _col(win_k_g[j]),
                                    cos_w, sin_w, n_lat_tiles)
            att = _win_attn(win_sink[j].reshape(-1), q, k2, v2, L, i < depth - 1)
            br = _pool(pz, _block_diag(pool_w[j]).astype(BF16), pool_scale[j][None, :], L)
            w_out = odd_w_out[j].astype(BF16)
            wa, wb = w_out[:C_QW][att_perm], w_out[C_QW:]
        ffn_args = (att, br, m, wa, wb, norm2_g[i][None, :], ffn_up[i].astype(BF16), ffn_conv_w[i],
                    ffn_conv_b[i][None, :], ffn_down[i].astype(BF16))
        if i < depth - 1:
            h_new = _mix_ffn(src[0], 0, *ffn_args, 0, L, FFN_LAT_TILE, False, L + C)
            h = _mix_ffn(src[1], src[2], *ffn_args, L, C, C, True, L + C, prev_out=h_new)
        else:
            h = _mix_ffn(src[0], 0, *ffn_args, 0, L, FFN_LAT_TILE, False, L)
    return h
```

```python
import functools
import math

import jax
import jax.numpy as jnp
from jax import lax
from jax.experimental import pallas as pl
from jax.experimental.pallas import tpu as pltpu

F32 = jnp.float32
BF16 = jnp.bfloat16

GRID_W = 64
N_MOD = 6
EPS = 1e-6
ROPE_BASE = 10000.0
NEG_INF = -1e30
A_HEADS = 12
A_NOPE = 64
A_ROPE = 32
A_QK = A_NOPE + A_ROPE
A_V = 64
A_Q_LORA = 256
A_KV_LORA = 128
A_IN = A_Q_LORA + A_KV_LORA + A_ROPE
B_GROUPS = 4
B_GDIM = 64
B_WIDTH = B_GROUPS * B_GDIM
C_HEADS = 12
C_KV_HEADS = 4
C_GROUP = C_HEADS // C_KV_HEADS
C_HDIM = 64
C_WINDOW = 128
C_QW = C_HEADS * C_HDIM
C_KW = C_KV_HEADS * C_HDIM
D_GROUPS = 4
D_GDIM = 64
D_WIDTH = D_GROUPS * D_GDIM
D_WINDOWS = (2, 4, 8, 16)

LANES = 128
BF16_ROWS = 16
HEAD_PAD = LANES
TOK_TILE = 256
FFN_HALO = 8
FFN_CHUNK = 256
FFN_ROW_BLOCK = 512
FFN_LAT_TILE = 1024
FNET_BGRP = 4
IN_TILE = 512
MLA_Q_TILE = 512
MLA_HEADS = 4
MLA_KEY_CHUNK = 256
MLA_PV_LAG = 1
LOG2_E = math.log2(math.e)
WIN_KEY_CHUNK = 256
VMEM_LIMIT = 56 * 1024 * 1024

NT_DIMS = (((1,), (1,)), ((), ()))


def _cparams(sem):
    return pltpu.CompilerParams(dimension_semantics=sem, vmem_limit_bytes=VMEM_LIMIT)


def _mod_norm(x, g, shift, scale):
    ms = jnp.mean(x * x, axis=-1, keepdims=True)
    y = x * lax.rsqrt(ms + EPS) * g
    return y * (1.0 + scale) + shift


def _rms_rows(xT, g_col):
    ms = jnp.mean(xT * xT, axis=0, keepdims=True)
    return xT * lax.rsqrt(ms + EPS) * g_col


def _rope_rows(x1, x2, cos, sin):
    return x1 * cos - x2 * sin, x1 * sin + x2 * cos


def _mod_kernel(cc_ref, w_ref, b_ref, o_ref):
    cc = cc_ref[...]
    a = (cc * jax.nn.sigmoid(cc)).astype(BF16)
    w = w_ref[0].astype(BF16)
    o_ref[0] = jnp.dot(a, w, preferred_element_type=F32) + b_ref[0]


def _modulation(cc, mod_w, mod_b):
    depth, d, n = mod_w.shape
    rows = cc.shape[0]
    tn = 1024
    return pl.pallas_call(
        _mod_kernel,
        grid=(depth, n // tn),
        in_specs=[
            pl.BlockSpec((rows, d), lambda i, j: (0, 0)),
            pl.BlockSpec((1, d, tn), lambda i, j: (i, 0, j)),
            pl.BlockSpec((1, 1, tn), lambda i, j: (i, 0, j)),
        ],
        out_specs=pl.BlockSpec((1, rows, tn), lambda i, j: (i, 0, j)),
        out_shape=jax.ShapeDtypeStruct((depth, rows, n), F32),
        compiler_params=_cparams(("arbitrary", "arbitrary")),
        name="modulation",
    )(cc, mod_w, mod_b.reshape(depth, 1, n))


def _even_in_kernel(x_ref, xc_ref, m_ref, g1_ref, w_in_ref, cqg_ref, wuq_ref, qg_ref, ckvg_ref, wukv_ref, kg_ref,
                    cos_ref, sin_ref, q_ref, k_ref, v_ref, z_ref, *, n_lat_tiles):
    tm = x_ref.shape[1]
    m = m_ref[0]
    xc = jnp.concatenate([xc_ref[0]] * (tm // xc_ref.shape[1]), axis=0)
    x = jnp.where(pl.program_id(1) < n_lat_tiles, x_ref[0], xc)
    u = _mod_norm(x, g1_ref[...], m[0:1], m[1:2]).astype(BF16)
    hT = lax.dot_general(w_in_ref[...], u, NT_DIMS, preferred_element_type=F32)
    cos = cos_ref[...]
    sin = sin_ref[...]
    scale = A_QK ** -0.5 * LOG2_E
    half = A_ROPE // 2
    zpad = jnp.zeros((HEAD_PAD - A_QK, tm), F32)
    ones_rows = jnp.ones((BF16_ROWS, tm), F32)

    z_ref[0] = hT[A_IN:A_IN + B_WIDTH].T.astype(BF16)

    cqn = _rms_rows(hT[0:A_Q_LORA], cqg_ref[...]).astype(BF16)
    qT = jnp.dot(wuq_ref[...], cqn, preferred_element_type=F32)
    qg = qg_ref[...] * scale
    for h in range(A_HEADS):
        qn = _rms_rows(qT[h * A_QK:(h + 1) * A_QK], qg)
        o1, o2 = _rope_rows(qn[A_NOPE:A_NOPE + half], qn[A_NOPE + half:A_QK], cos, sin)
        full = jnp.concatenate([qn[0:A_NOPE], o1, o2, zpad], axis=0)
        q_ref[0, h] = full.astype(BF16)

    ckvn = _rms_rows(hT[A_Q_LORA:A_Q_LORA + A_KV_LORA], ckvg_ref[...]).astype(BF16)
    kvT = jnp.dot(wukv_ref[...], ckvn, preferred_element_type=F32)
    kr = hT[A_Q_LORA + A_KV_LORA:A_IN]
    kr_ss = jnp.sum(kr * kr, axis=0, keepdims=True)
    kg = kg_ref[...]
    hw = A_NOPE + A_V
    for h in range(A_HEADS):
        kn = kvT[h * hw:h * hw + A_NOPE]
        r = lax.rsqrt((jnp.sum(kn * kn, axis=0, keepdims=True) + kr_ss) * (1.0 / A_QK) + EPS)
        knn = kn * r * kg[0:A_NOPE]
        krn = kr * r * kg[A_NOPE:A_QK]
        o1, o2 = _rope_rows(krn[0:half], krn[half:A_ROPE], cos, sin)
        k_ref[0, h] = jnp.concatenate([knn, o1, o2, zpad], axis=0).T.astype(BF16)
        vh = kvT[h * hw + A_NOPE:(h + 1) * hw]
        v_ref[0, h] = jnp.concatenate([vh, ones_rows], axis=0).astype(BF16)


def _even_in(x, xc, xc_row0, n_ctx, m, g1, w_inT, cq_g, w_uqT, q_g, ckv_g, w_ukvT, k_g, cosT, sinT, n_lat):
    B, _, D = x.shape
    Lt = n_lat + n_ctx
    tm = IN_TILE
    assert n_lat % tm == 0 and tm % n_ctx == 0 and xc_row0 % n_ctx == 0
    nt = pl.cdiv(Lt, tm)
    n_lat_tiles = n_lat // tm
    full2 = lambda a: pl.BlockSpec(a.shape, lambda b, t: (0, 0))
    return pl.pallas_call(
        functools.partial(_even_in_kernel, n_lat_tiles=n_lat_tiles),
        grid=(B, nt),
        in_specs=[
            pl.BlockSpec((1, tm, D), lambda b, t: (b, jnp.minimum(t, n_lat_tiles - 1), 0)),
            pl.BlockSpec((1, n_ctx, D), lambda b, t: (b, xc_row0 // n_ctx, 0)),
            pl.BlockSpec((1, N_MOD, D), lambda b, t: (jnp.where(t < n_lat_tiles, b, B), 0, 0)),
            full2(g1), full2(w_inT), full2(cq_g), full2(w_uqT), full2(q_g), full2(ckv_g), full2(w_ukvT), full2(k_g),
            pl.BlockSpec((A_ROPE // 2, tm), lambda b, t: (0, t)),
            pl.BlockSpec((A_ROPE // 2, tm), lambda b, t: (0, t)),
        ],
        out_specs=[
            pl.BlockSpec((1, A_HEADS, HEAD_PAD, tm), lambda b, t: (b, 0, 0, t)),
            pl.BlockSpec((1, A_HEADS, tm, HEAD_PAD), lambda b, t: (b, 0, t, 0)),
            pl.BlockSpec((1, A_HEADS, A_V + BF16_ROWS, tm), lambda b, t: (b, 0, 0, t)),
            pl.BlockSpec((1, tm, B_WIDTH), lambda b, t: (b, t, 0)),
        ],
        out_shape=[
            jax.ShapeDtypeStruct((B, A_HEADS, HEAD_PAD, Lt), BF16),
            jax.ShapeDtypeStruct((B, A_HEADS, Lt, HEAD_PAD), BF16),
            jax.ShapeDtypeStruct((B, A_HEADS, A_V + BF16_ROWS, Lt), BF16),
            jax.ShapeDtypeStruct((B, Lt, B_WIDTH), BF16),
        ],
        compiler_params=_cparams(("parallel", "arbitrary")),
        name="even_in",
    )(x, xc, m, g1, w_inT, cq_g, w_uqT, q_g, ckv_g, w_ukvT, k_g, cosT, sinT)


def _mla_attn_kernel(*refs, aliased):
    q_ref, k_ref, v_ref, o_ref = refs[1:] if aliased else refs
    nk = k_ref.shape[2]
    nh = q_ref.shape[1]
    qs = [q_ref[0, j] for j in range(nh)]
    m = [None] * nh
    acc = [None] * nh
    pend = [[] for _ in range(nh)]
    chunks = [(k0, min(k0 + MLA_KEY_CHUNK, nk)) for k0 in range(0, nk, MLA_KEY_CHUNK)]

    def scores(j, k0, k1):
        s = jnp.dot(k_ref[0, j, k0:k1, :], qs[j], preferred_element_type=F32)
        mx = jnp.max(s, axis=0, keepdims=True)
        m_new = mx if m[j] is None else jnp.maximum(m[j], mx)
        alpha = None if m[j] is None else jnp.exp2(m[j] - m_new)
        pend[j].append((jnp.exp2(s - m_new).astype(BF16), alpha, k0, k1))
        m[j] = m_new

    def weighted_values(j):
        p, alpha, k0, k1 = pend[j].pop(0)
        pv = jnp.dot(v_ref[0, j, :, k0:k1], p, preferred_element_type=F32)
        acc[j] = pv if alpha is None else alpha * acc[j] + pv

    for i in range(len(chunks) + MLA_PV_LAG):
        for j in range(nh):
            if i < len(chunks):
                scores(j, *chunks[i])
            if i >= MLA_PV_LAG:
                weighted_values(j)
    outs = [(a / a[A_V:A_V + 1])[0:A_V] for a in acc]
    o_ref[0] = jnp.concatenate(outs, axis=0).T.astype(o_ref.dtype)


def _mla_attn(q, kT, vT, q_row0, n_q, tq, k_col0, n_k, nh, prev_out=None):
    B, H, _, Lt = q.shape
    assert q_row0 % tq == 0 and n_q % tq == 0 and k_col0 % n_k == 0 and H % nh == 0 and (nh * A_V) % LANES == 0
    qb, kb = q_row0 // tq, k_col0 // n_k
    aliased = prev_out is not None
    in_specs = [
        pl.BlockSpec((1, nh, HEAD_PAD, tq), lambda b, h, t: (b, h, 0, qb + t)),
        pl.BlockSpec((1, nh, n_k, HEAD_PAD), lambda b, h, t: (b, h, kb, 0)),
        pl.BlockSpec((1, nh, A_V + BF16_ROWS, n_k), lambda b, h, t: (b, h, 0, kb)),
    ]
    args = [q, kT, vT]
    if aliased:
        in_specs = [pl.BlockSpec(memory_space=pl.ANY)] + in_specs
        args = [prev_out] + args
    return pl.pallas_call(
        functools.partial(_mla_attn_kernel, aliased=aliased),
        grid=(B, H // nh, n_q // tq),
        in_specs=in_specs,
        out_specs=pl.BlockSpec((1, tq, nh * A_V), lambda b, h, t: (b, qb + t, h)),
        out_shape=jax.ShapeDtypeStruct((B, Lt, H * A_V), BF16),
        input_output_aliases={0: 0} if aliased else {},
        compiler_params=_cparams(("parallel", "arbitrary", "arbitrary")),
        name="mla_attn",
    )(*args)


def _fnet_kernel(*refs, norm, aliased):
    if aliased:
        refs = refs[1:]
    c_ref, s_ref, z_ref, cs_ref, wf_ref, o_ref, p_acc, q_acc = refs
    k = pl.program_id(2)

    @pl.when(k == 0)
    def _():
        p_acc[...] = jnp.zeros_like(p_acc)
        q_acc[...] = jnp.zeros_like(q_acc)

    c = c_ref[...]
    s = s_ref[...]
    for j in range(z_ref.shape[0]):
        z = z_ref[j]
        p_acc[j] += jnp.dot(c, z, preferred_element_type=F32)
        q_acc[j] += jnp.dot(s, z, preferred_element_type=F32)

    @pl.when(k == pl.num_programs(2) - 1)
    def _():
        for j in range(z_ref.shape[0]):
            pq = jnp.concatenate([p_acc[j], q_acc[j]], axis=1).astype(BF16)
            y = jnp.dot(pq, cs_ref[...], preferred_element_type=F32) * norm
            o_ref[j] = jnp.dot(y.astype(BF16), wf_ref[...], preferred_element_type=F32).astype(o_ref.dtype)


def _fnet(z, c_tab, s_tab, cs_bd, wf_bd, row0, out_rows, prev_out=None):
    B = z.shape[0]
    n = c_tab.shape[0]
    tm = min(n, 1024)
    tk = min(n, 1024)
    g = FNET_BGRP
    norm = 1.0 / math.sqrt(n * B_GDIM)
    aliased = prev_out is not None
    in_specs = [
        pl.BlockSpec((tm, tk), lambda b, i, k: (i, k)),
        pl.BlockSpec((tm, tk), lambda b, i, k: (i, k)),
        pl.BlockSpec((g, tk, B_WIDTH), lambda b, i, k: (b, row0 // tk + k, 0)),
        pl.BlockSpec(cs_bd.shape, lambda b, i, k: (0, 0)),
        pl.BlockSpec(wf_bd.shape, lambda b, i, k: (0, 0)),
    ]
    args = [c_tab, s_tab, z, cs_bd, wf_bd]
    if aliased:
        in_specs = [pl.BlockSpec(memory_space=pl.ANY)] + in_specs
        args = [prev_out] + args
    return pl.pallas_call(
        functools.partial(_fnet_kernel, norm=norm, aliased=aliased),
        grid=(B // g, n // tm, n // tk),
        in_specs=in_specs,
        out_specs=pl.BlockSpec((g, tm, B_WIDTH), lambda b, i, k: (b, row0 // tm + i, 0)),
        out_shape=jax.ShapeDtypeStruct((B, out_rows, B_WIDTH), BF16),
        scratch_shapes=[pltpu.VMEM((g, tm, B_WIDTH), F32), pltpu.VMEM((g, tm, B_WIDTH), F32)],
        input_output_aliases={0: 0} if aliased else {},
        compiler_params=_cparams(("parallel", "arbitrary", "arbitrary")),
        name="fnet",
    )(*args)


def _odd_in_kernel(x_ref, m_ref, g1_ref, w_in_ref, qg_ref, kg_ref, cos_ref, sin_ref, q_ref, k_ref, v_ref, pz_ref):
    tm = x_ref.shape[1]
    m = m_ref[0]
    u = _mod_norm(x_ref[0], g1_ref[...], m[0:1], m[1:2]).astype(BF16)
    hT = lax.dot_general(w_in_ref[...], u, NT_DIMS, preferred_element_type=F32)
    cos = cos_ref[...]
    sin = sin_ref[...]
    half = C_HDIM // 2
    zeros = jnp.zeros((C_HDIM, tm), F32)

    def norm_rope(xT, g_col):
        xn = _rms_rows(xT, g_col)
        o1, o2 = _rope_rows(xn[0:half], xn[half:C_HDIM], cos, sin)
        return jnp.concatenate([o1, o2], axis=0)

    qg = qg_ref[...] * (C_HDIM ** -0.5 * LOG2_E)
    for h in range(C_HEADS):
        qh = norm_rope(hT[h * C_HDIM:(h + 1) * C_HDIM], qg)
        parts = [qh, zeros] if (h // C_GROUP) % 2 == 0 else [zeros, qh]
        q_ref[0, h] = jnp.concatenate(parts, axis=0).astype(BF16)

    kg = kg_ref[...]
    for p in range(C_KV_HEADS // 2):
        ks = [norm_rope(hT[C_QW + (2 * p + e) * C_HDIM:C_QW + (2 * p + e + 1) * C_HDIM], kg) for e in range(2)]
        k_ref[0, p] = jnp.concatenate(ks, axis=0).T.astype(BF16)
        v0 = C_QW + C_KW + 2 * p * C_HDIM
        v_ref[0, p] = hT[v0:v0 + 2 * C_HDIM].T.astype(BF16)

    pz_ref[0] = hT[C_QW + 2 * C_KW:C_QW + 2 * C_KW + D_WIDTH].T


def _odd_in(x, m, g1, w_inT, q_g, k_g, cosT, sinT, n_lat_tiles):
    B, Lt, D = x.shape
    tm = IN_TILE
    nt = pl.cdiv(Lt, tm)
    n_lat_tiles = (n_lat_tiles * TOK_TILE) // tm
    full2 = lambda a: pl.BlockSpec(a.shape, lambda b, t: (0, 0))
    np_ = C_KV_HEADS // 2
    return pl.pallas_call(
        _odd_in_kernel,
        grid=(B, nt),
        in_specs=[
            pl.BlockSpec((1, tm, D), lambda b, t: (b, t, 0)),
            pl.BlockSpec((1, N_MOD, D), lambda b, t: (jnp.where(t < n_lat_tiles, b, B), 0, 0)),
            full2(g1), full2(w_inT), full2(q_g), full2(k_g),
            pl.BlockSpec((C_HDIM // 2, tm), lambda b, t: (0, t)),
            pl.BlockSpec((C_HDIM // 2, tm), lambda b, t: (0, t)),
        ],
        out_specs=[
            pl.BlockSpec((1, C_HEADS, HEAD_PAD, tm), lambda b, t: (b, 0, 0, t)),
            pl.BlockSpec((1, np_, tm, HEAD_PAD), lambda b, t: (b, 0, t, 0)),
            pl.BlockSpec((1, np_, tm, HEAD_PAD), lambda b, t: (b, 0, t, 0)),
            pl.BlockSpec((1, tm, D_WIDTH), lambda b, t: (b, t, 0)),
        ],
        out_shape=[
            jax.ShapeDtypeStruct((B, C_HEADS, HEAD_PAD, Lt), BF16),
            jax.ShapeDtypeStruct((B, np_, Lt, HEAD_PAD), BF16),
            jax.ShapeDtypeStruct((B, np_, Lt, HEAD_PAD), BF16),
            jax.ShapeDtypeStruct((B, Lt, D_WIDTH), F32),
        ],
        compiler_params=_cparams(("parallel", "arbitrary")),
        name="odd_in",
    )(x, m, g1, w_inT, q_g, k_g, cosT, sinT)


def _win_attn_kernel(sink_ref, q_ref, k_ref, v_ref, o_ref, *, n_lat_tiles, n_lat):
    t = pl.program_id(1)
    npair = k_ref.shape[1]
    nh = q_ref.shape[1] // npair
    tq = q_ref.shape[3]
    half = nh // 2
    ck = WIN_KEY_CHUNK
    col = lax.broadcasted_iota(jnp.int32, (1, 2 * tq), 1)
    row = lax.broadcasted_iota(jnp.int32, (HEAD_PAD, tq), 0)

    def run(pair, chunks):
        qg = [jnp.concatenate([q_ref[0, pair * nh + j], q_ref[0, pair * nh + half + j]], axis=1)
              for j in range(half)]
        sk = [jnp.where(col < tq, sink_ref[pair * nh + j], sink_ref[pair * nh + half + j]) * LOG2_E
              for j in range(half)]
        m = list(sk)
        acc = [None] * half
        pend = [[] for _ in range(half)]

        def scores(j, kb, valid):
            s = jnp.dot(kb, qg[j], preferred_element_type=F32)
            if valid is not None:
                s = jnp.where(valid, s, NEG_INF)
            m_new = jnp.maximum(m[j], jnp.max(s, axis=0, keepdims=True))
            pend[j].append((jnp.exp2(s - m_new).astype(BF16), jnp.exp2(m[j] - m_new)))
            m[j] = m_new

        def weighted_values(j, vbT):
            p, alpha = pend[j].pop(0)
            pv = jnp.dot(vbT, p, preferred_element_type=F32)
            acc[j] = pv if acc[j] is None else alpha * acc[j] + pv

        for i in range(len(chunks) + 1):
            for j in range(half):
                if i < len(chunks):
                    scores(j, chunks[i][0], chunks[i][2])
                if i >= 1:
                    weighted_values(j, chunks[i - 1][1])
        for j in range(half):
            l = acc[j][HEAD_PAD:HEAD_PAD + 1] + jnp.exp2(sk[j] - m[j])
            o = acc[j][0:HEAD_PAD] / l
            blk = jnp.where(row < C_HDIM, o[:, 0:tq], o[:, tq:2 * tq])
            c0 = (pair * half + j) * HEAD_PAD
            o_ref[0, :, c0:c0 + HEAD_PAD] = blk.T.astype(o_ref.dtype)

    def kv_chunk(pair, start):
        kb = k_ref[0, pair, pl.ds(start, ck), :]
        vT = v_ref[0, pair, pl.ds(start, ck), :].astype(F32).T
        return kb, jnp.concatenate([vT, jnp.ones((BF16_ROWS, ck), F32)], axis=0).astype(BF16)

    @pl.when(t < n_lat_tiles)
    def _():
        start = pl.multiple_of(jnp.maximum(t * tq - C_WINDOW, 0), C_WINDOW)
        qpos = t * tq + (lax.broadcasted_iota(jnp.int32, (ck, 2 * tq), 1) & (tq - 1))
        valids = []
        for c in range((tq + 2 * C_WINDOW) // ck):
            kpos = start + c * ck + lax.broadcasted_iota(jnp.int32, (ck, 2 * tq), 0)
            valids.append((jnp.abs(kpos - qpos) <= C_WINDOW) & (kpos < n_lat))
        for pair in range(npair):
            chunks = [kv_chunk(pair, start + c * ck) + (valid,) for c, valid in enumerate(valids)]
            run(pair, chunks + [kv_chunk(pair, n_lat) + (None,)])

    @pl.when(t >= n_lat_tiles)
    def _():
        for pair in range(npair):
            run(pair, [kv_chunk(pair, n_lat) + (None,)])


def _win_attn(sink, q, k2, v2, n_lat, ctx_queries):
    B, H, _, Lt = q.shape
    tq = TOK_TILE
    npair = k2.shape[1]
    return pl.pallas_call(
        functools.partial(_win_attn_kernel, n_lat_tiles=n_lat // tq, n_lat=n_lat),
        grid=(B, (Lt if ctx_queries else n_lat) // tq),
        in_specs=[
            pl.BlockSpec(memory_space=pltpu.SMEM),
            pl.BlockSpec((1, H, HEAD_PAD, tq), lambda b, t: (b, 0, 0, t)),
            pl.BlockSpec((1, npair, Lt, HEAD_PAD), lambda b, t: (b, 0, 0, 0)),
            pl.BlockSpec((1, npair, Lt, HEAD_PAD), lambda b, t: (b, 0, 0, 0)),
        ],
        out_specs=pl.BlockSpec((1, tq, H * C_HDIM), lambda b, t: (b, t, 0)),
        out_shape=jax.ShapeDtypeStruct((B, Lt, H * C_HDIM), BF16),
        compiler_params=_cparams(("parallel", "arbitrary")),
        name="win_attn",
    )(sink, q, k2, v2)


def _pool_lane_tile(z, wp, ps, g0):
    n = z.shape[0]
    row = lax.broadcasted_iota(jnp.int32, z.shape, 0)
    low = lax.broadcasted_iota(jnp.int32, z.shape, 1) < D_GDIM

    def shift_down(a, s):
        return jnp.where(row >= s, pltpu.roll(a, s, 0), 0.0)

    def shift_up(a, s):
        return jnp.where(row < n - s, pltpu.roll(a, n - s, 0), 0.0)

    h_lo, h_hi = D_WINDOWS[g0] // 2, D_WINDOWS[g0 + 1] // 2
    back, fwd, h = z, z, 1
    levels = {1: (z, z)}
    while h < h_hi:
        back = back + shift_down(back, h)
        fwd = fwd + shift_up(fwd, h)
        h *= 2
        levels[h] = (back, fwd)
    back_sel = jnp.where(low, levels[h_lo][0], levels[h_hi][0])
    fwd_sel = jnp.where(low, levels[h_lo][1], levels[h_hi][1])
    win_sum = shift_down(back_sel, 1) + fwd_sel
    half = jnp.where(low, h_lo, h_hi)
    cnt = jnp.minimum(row + half, n) - jnp.maximum(row - half, 0)
    pooled = (win_sum / cnt.astype(F32) - z).astype(BF16)
    return jnp.dot(pooled, wp, preferred_element_type=F32) * ps


def _pool_kernel(pz_ref, wp_ref, ps_ref, o_ref, *, n_lat):
    lt = pz_ref.shape[1]
    for lo, hi in ((0, n_lat), (n_lat, lt)):
        for c in range(D_GROUPS // 2):
            c0, c1 = c * LANES, (c + 1) * LANES
            y = _pool_lane_tile(pz_ref[0, lo:hi, c0:c1], wp_ref[c0:c1, c0:c1], ps_ref[:, c0:c1], 2 * c)
            o_ref[0, lo:hi, c0:c1] = y.astype(o_ref.dtype)


def _pool(pz, wp_bd, p_scale, n_lat):
    B, Lt, W = pz.shape
    return pl.pallas_call(
        functools.partial(_pool_kernel, n_lat=n_lat),
        grid=(B,),
        in_specs=[
            pl.BlockSpec((1, Lt, W), lambda b: (b, 0, 0)),
            pl.BlockSpec(wp_bd.shape, lambda b: (0, 0)),
            pl.BlockSpec(p_scale.shape, lambda b: (0, 0)),
        ],
        out_specs=pl.BlockSpec((1, Lt, W), lambda b: (b, 0, 0)),
        out_shape=jax.ShapeDtypeStruct((B, Lt, W), BF16),
        compiler_params=_cparams(("parallel",)),
        name="pool",
    )(pz, wp_bd, p_scale)


def _mix_ffn_kernel(*refs, d_ff, aliased):
    if aliased:
        refs = refs[1:]
    (x_ref, xp_ref, xn_ref, a_ref, ap_ref, an_ref, b_ref, bp_ref, bn_ref, m_ref, g2_ref, wa_ref, wb_ref,
     wup_ref, cw_ref, cb_ref, wdn_ref, o_ref, a_scr, b_scr, x1_scr, u_scr, act_scr) = refs
    t = pl.program_id(1)
    tm = x_ref.shape[1]
    hx = FFN_HALO
    hm = BF16_ROWS
    rows = tm + 2 * hx
    m = m_ref[0]
    g2 = g2_ref[...]

    for src, prv, nxt, scr in ((a_ref, ap_ref, an_ref, a_scr), (b_ref, bp_ref, bn_ref, b_scr)):
        scr[0:tm] = src[0]
        scr[tm:tm + hm] = nxt[0]
        scr[tm + hm:tm + 2 * hm] = prv[0]
    gate1 = m[2:3]

    def mod_norm_rows(xv):
        return _mod_norm(xv, g2, m[3:4], m[4:5])

    def out_proj(r0, r1):
        return (jnp.dot(a_scr[r0:r1], wa_ref[...], preferred_element_type=F32)
                + jnp.dot(b_scr[r0:r1], wb_ref[...], preferred_element_type=F32))

    nblk = max(tm // FFN_ROW_BLOCK, 1)
    rb = tm // nblk
    for i in range(nblk):
        r0, r1 = i * rb, (i + 1) * rb
        x1 = x_ref[0, r0:r1, :] + gate1 * out_proj(r0, r1)
        x1_scr[r0:r1] = x1
        u_scr[r0:r1] = mod_norm_rows(x1).astype(BF16)
    yh = out_proj(tm, tm + 2 * hm)
    x1_next = xn_ref[0] + gate1 * yh[0:hx]
    x1_prev = xp_ref[0] + gate1 * yh[2 * hm - hx:2 * hm]
    u_next = jnp.where(t == pl.num_programs(1) - 1, 0.0, mod_norm_rows(x1_next))
    u_prev = jnp.where(t == 0, 0.0, mod_norm_rows(x1_prev))
    u_scr[tm:rows] = jnp.concatenate([u_next, u_prev], axis=0).astype(BF16)

    def conv(h, col0, width):
        cw = cw_ref[:, col0:col0 + width]
        out = (pltpu.roll(h, 1, 0) * cw[0:1] + h * cw[1:2] + pltpu.roll(h, rows - 1, 0) * cw[2:3]
               + cb_ref[:, col0:col0 + width])
        return out[0:tm]

    u = u_scr[...]
    for c in range(d_ff // FFN_CHUNK):
        c0 = c * FFN_CHUNK
        gate = conv(jnp.dot(u, wup_ref[:, c0:c0 + FFN_CHUNK], preferred_element_type=F32), c0, FFN_CHUNK)
        val = conv(jnp.dot(u, wup_ref[:, d_ff + c0:d_ff + c0 + FFN_CHUNK], preferred_element_type=F32),
                   d_ff + c0, FFN_CHUNK)
        act_scr[:, c0:c0 + FFN_CHUNK] = (gate * jax.nn.sigmoid(gate) * val).astype(BF16)

    for i in range(nblk):
        r0, r1 = i * rb, (i + 1) * rb
        y2 = jnp.dot(act_scr[r0:r1], wdn_ref[...], preferred_element_type=F32)
        o_ref[0, r0:r1, :] = x1_scr[r0:r1] + m[5:6] * y2


def _mix_ffn(x, x_row0, att, br, m, wa, wb, g2, w_up, conv_w, conv_b, w_down, row0, n_rows, tm, ctx_mod, out_rows,
             prev_out=None):
    B, x_rows, D = x.shape
    Lt = att.shape[1]
    d_ff = w_down.shape[0]
    assert row0 % tm == 0 and x_row0 % tm == 0 and n_rows % tm == 0 and tm % BF16_ROWS == 0
    assert 2 * FFN_HALO == BF16_ROWS

    def tile(w, r0=row0):
        return pl.BlockSpec((1, tm, w), lambda b, t: (b, r0 // tm + t, 0))

    def halo(w, h, prev, r0=row0, total=Lt):
        per, first, n = tm // h, r0 // h, total // h
        if prev:
            return pl.BlockSpec((1, h, w), lambda b, t: (b, jnp.maximum(first + t * per - 1, 0), 0))
        return pl.BlockSpec((1, h, w), lambda b, t: (b, jnp.minimum(first + (t + 1) * per, n - 1), 0))

    const = lambda a: pl.BlockSpec(a.shape, lambda b, t: (0, 0))
    once = lambda a: pl.BlockSpec(a.shape, lambda b, t: (0, 0), pipeline_mode=pl.Buffered(1))
    wa_w, wb_w = att.shape[2], br.shape[2]
    aliased = prev_out is not None
    in_specs = [
        tile(D, x_row0), halo(D, FFN_HALO, True, x_row0, x_rows), halo(D, FFN_HALO, False, x_row0, x_rows),
        tile(wa_w), halo(wa_w, BF16_ROWS, True), halo(wa_w, BF16_ROWS, False),
        tile(wb_w), halo(wb_w, BF16_ROWS, True), halo(wb_w, BF16_ROWS, False),
        pl.BlockSpec((1, N_MOD, D), (lambda b, t: (B, 0, 0)) if ctx_mod else (lambda b, t: (b, 0, 0))),
        const(g2), once(wa), once(wb), once(w_up), const(conv_w), const(conv_b), once(w_down),
    ]
    args = [x, x, x, att, att, att, br, br, br, m, g2, wa, wb, w_up, conv_w, conv_b, w_down]
    if aliased:
        in_specs = [pl.BlockSpec(memory_space=pl.ANY)] + in_specs
        args = [prev_out] + args
    return pl.pallas_call(
        functools.partial(_mix_ffn_kernel, d_ff=d_ff, aliased=aliased),
        grid=(B, n_rows // tm),
        in_specs=in_specs,
        out_specs=tile(D),
        out_shape=jax.ShapeDtypeStruct((B, out_rows, D), F32),
        scratch_shapes=[
            pltpu.VMEM((tm + 2 * BF16_ROWS, wa_w), BF16), pltpu.VMEM((tm + 2 * BF16_ROWS, wb_w), BF16),
            pltpu.VMEM((tm, D), F32), pltpu.VMEM((tm + 2 * FFN_HALO, D), BF16), pltpu.VMEM((tm, d_ff), BF16)],
        input_output_aliases={0: 0} if aliased else {},
        compiler_params=_cparams(("parallel", "arbitrary")),
        name="mix_ffn",
    )(*args)


def _rope_tables_T(rows, rot_dim, n_ctx):
    n_freq = rot_dim // 4
    inv = ROPE_BASE ** (-jnp.arange(n_freq, dtype=F32) / n_freq)
    row = jnp.broadcast_to(jnp.arange(rows, dtype=F32)[:, None], (rows, GRID_W)).reshape(-1)
    col = jnp.broadcast_to(jnp.arange(GRID_W, dtype=F32)[None, :], (rows, GRID_W)).reshape(-1)
    ang = jnp.concatenate([row[:, None] * inv, col[:, None] * inv], axis=-1)
    cosT = jnp.concatenate([jnp.cos(ang).T, jnp.ones((rot_dim // 2, n_ctx), F32)], axis=1)
    sinT = jnp.concatenate([jnp.sin(ang).T, jnp.zeros((rot_dim // 2, n_ctx), F32)], axis=1)
    return cosT, sinT


def _dft_tables(n):
    idx = (jnp.arange(n, dtype=jnp.int32)[:, None] * jnp.arange(n, dtype=jnp.int32)[None, :]) % n
    ang = idx.astype(F32) * (2.0 * math.pi / n)
    return jnp.cos(ang), jnp.sin(ang)


def _dft_tables_split(n, r):
    l = jnp.arange(n, dtype=jnp.int32)[:, None]
    kk = jnp.arange(r, dtype=jnp.int32)[None, :]
    a = ((kk * l) % r).astype(F32) * (2.0 * math.pi / r)
    b = ((kk * l) % n).astype(F32) * (2.0 * math.pi / n)
    ca, sa, cb, sb = jnp.cos(a)[:, :, None], jnp.sin(a)[:, :, None], jnp.cos(b)[:, None, :], jnp.sin(b)[:, None, :]
    return (ca * cb - sa * sb).reshape(n, n), (sa * cb + ca * sb).reshape(n, n)


def _block_diag(w):
    g, a, b = w.shape
    eye = jnp.eye(g, dtype=w.dtype)
    return (eye[:, None, :, None] * w[:, :, None, :]).reshape(g * a, g * b)


def _col(v):
    return v.reshape(-1, 1).astype(F32)


def kernel(x, c, ctx, c_ctx, mod_w, mod_b, norm1_g, norm2_g, mla_w_in, mla_cq_g, mla_ckv_g, mla_w_uq, mla_w_ukv,
           mla_q_g, mla_k_g, fnet_w, even_w_out, win_w_in, win_q_g, win_k_g, win_sink, pool_w, pool_scale, odd_w_out,
           ffn_up, ffn_conv_w, ffn_conv_b, ffn_down):
    B, L, D = x.shape
    C = ctx.shape[1]
    depth = mod_w.shape[0]
    assert L % TOK_TILE == 0 and C == TOK_TILE and B % FNET_BGRP == 0
    n_lat_tiles = L // TOK_TILE
    rows = L // GRID_W

    pad = (-(B + 1)) % 8
    cc = jnp.concatenate([c, c_ctx[None, :], jnp.zeros((pad, D), F32)], axis=0)
    mods = _modulation(cc, mod_w, mod_b).reshape(depth, B + 1 + pad, N_MOD, D)

    cos_a, sin_a = _rope_tables_T(rows, A_ROPE, C)
    cos_w, sin_w = _rope_tables_T(rows, C_HDIM, C)
    r_lat = math.isqrt(L)
    c_lat, s_lat = (t.astype(BF16) for t in (_dft_tables_split(L, r_lat) if r_lat * r_lat == L else _dft_tables(L)))
    c_ctx_t, s_ctx_t = (t.astype(BF16) for t in _dft_tables(C))
    cc64, ss64 = _dft_tables(B_GDIM)
    eye_g = jnp.eye(B_GROUPS, dtype=F32)
    cs_bd = jnp.concatenate([jnp.kron(eye_g, cc64), -jnp.kron(eye_g, ss64)], axis=0).astype(BF16)

    nh = C_HEADS // (C_KV_HEADS // 2)
    head_order = [p * nh + e * (nh // 2) + j for p in range(C_KV_HEADS // 2) for j in range(nh // 2) for e in range(2)]
    att_perm = jnp.asarray([h * C_HDIM + d for h in head_order for d in range(C_HDIM)], jnp.int32)

    h = None
    for i in range(depth):
        j = i // 2
        m = mods[i]
        g1 = norm1_g[i][None, :]
        src = (x, ctx, 0) if h is None else (h, h, L)
        if i % 2 == 0:
            q, kT, vT, z = _even_in(
                *src, C, m, g1, mla_w_in[j].T.astype(BF16), _col(mla_cq_g[j]), mla_w_uq[j].T.astype(BF16),
                _col(mla_q_g[j]), _col(mla_ckv_g[j]), mla_w_ukv[j].T.astype(BF16), _col(mla_k_g[j]),
                cos_a, sin_a, L)
            att = _mla_attn(q, kT, vT, 0, L, MLA_Q_TILE, 0, L + C, MLA_HEADS)
            att = _mla_attn(q, kT, vT, L, C, C, L, C, MLA_HEADS, prev_out=att)
            wf_bd = _block_diag(fnet_w[j]).astype(BF16)
            br = _fnet(z, c_lat, s_lat, cs_bd, wf_bd, 0, L + C)
            br = _fnet(z, c_ctx_t, s_ctx_t, cs_bd, wf_bd, L, L + C, prev_out=br)
            w_out = even_w_out[j].astype(BF16)
            wa, wb = w_out[:A_HEADS * A_V], w_out[A_HEADS * A_V:]
        else:
            q, k2, v2, pz = _odd_in(h, m, g1, win_w_in[j].T.astype(BF16), _col(win_q_g[j]), _col(win_k_g[j]),
                                    cos_w, sin_w, n_lat_tiles)
            att = _win_attn(win_sink[j].reshape(-1), q, k2, v2, L, i < depth - 1)
            br = _pool(pz, _block_diag(pool_w[j]).astype(BF16), pool_scale[j][None, :], L)
            w_out = odd_w_out[j].astype(BF16)
            wa, wb = w_out[:C_QW][att_perm], w_out[C_QW:]
        ffn_args = (att, br, m, wa, wb, norm2_g[i][None, :], ffn_up[i].astype(BF16), ffn_conv_w[i],
                    ffn_conv_b[i][None, :], ffn_down[i].astype(BF16))
        if i < depth - 1:
            h_new = _mix_ffn(src[0], 0, *ffn_args, 0, L, FFN_LAT_TILE, False, L + C)
            h = _mix_ffn(src[1], src[2], *ffn_args, L, C, C, True, L + C, prev_out=h_new)
        else:
            h = _mix_ffn(src[0], 0, *ffn_args, 0, L, FFN_LAT_TILE, False, L)
    return h
```

```python
import functools
import math

import jax
import jax.numpy as jnp
from jax import lax
from jax.experimental import pallas as pl
from jax.experimental.pallas import tpu as pltpu

F32 = jnp.float32
BF16 = jnp.bfloat16

GRID_W = 64
N_MOD = 6
EPS = 1e-6
ROPE_BASE = 10000.0
NEG_INF = -1e30
A_HEADS = 12
A_NOPE = 64
A_ROPE = 32
A_QK = A_NOPE + A_ROPE
A_V = 64
A_Q_LORA = 256
A_KV_LORA = 128
A_IN = A_Q_LORA + A_KV_LORA + A_ROPE
B_GROUPS = 4
B_GDIM = 64
B_WIDTH = B_GROUPS * B_GDIM
C_HEADS = 12
C_KV_HEADS = 4
C_GROUP = C_HEADS // C_KV_HEADS
C_HDIM = 64
C_WINDOW = 128
C_QW = C_HEADS * C_HDIM
C_KW = C_KV_HEADS * C_HDIM
D_GROUPS = 4
D_GDIM = 64
D_WIDTH = D_GROUPS * D_GDIM
D_WINDOWS = (2, 4, 8, 16)

LANES = 128
BF16_ROWS = 16
HEAD_PAD = LANES
TOK_TILE = 256
FFN_HALO = 8
FFN_CHUNK = 256
FFN_ROW_BLOCK = 512
FFN_LAT_TILE = 1024
FNET_BGRP = 4
IN_TILE = 512
MLA_Q_TILE = 512
MLA_HEADS = 2
MLA_KEY_CHUNK = 256
MLA_PV_LAG = 1
LOG2_E = math.log2(math.e)
WIN_KEY_CHUNK = 256
VMEM_LIMIT = 56 * 1024 * 1024

NT_DIMS = (((1,), (1,)), ((), ()))


def _cparams(sem):
    return pltpu.CompilerParams(dimension_semantics=sem, vmem_limit_bytes=VMEM_LIMIT)


def _mod_norm(x, g, shift, scale):
    ms = jnp.mean(x * x, axis=-1, keepdims=True)
    y = x * lax.rsqrt(ms + EPS) * g
    return y * (1.0 + scale) + shift


def _rms_rows(xT, g_col):
    ms = jnp.mean(xT * xT, axis=0, keepdims=True)
    return xT * lax.rsqrt(ms + EPS) * g_col


def _rope_rows(x1, x2, cos, sin):
    return x1 * cos - x2 * sin, x1 * sin + x2 * cos


def _mod_kernel(cc_ref, w_ref, b_ref, o_ref):
    cc = cc_ref[...]
    a = (cc * jax.nn.sigmoid(cc)).astype(BF16)
    w = w_ref[0].astype(BF16)
    o_ref[0] = jnp.dot(a, w, preferred_element_type=F32) + b_ref[0]


def _modulation(cc, mod_w, mod_b):
    depth, d, n = mod_w.shape
    rows = cc.shape[0]
    tn = 1024
    return pl.pallas_call(
        _mod_kernel,
        grid=(depth, n // tn),
        in_specs=[
            pl.BlockSpec((rows, d), lambda i, j: (0, 0)),
            pl.BlockSpec((1, d, tn), lambda i, j: (i, 0, j)),
            pl.BlockSpec((1, 1, tn), lambda i, j: (i, 0, j)),
        ],
        out_specs=pl.BlockSpec((1, rows, tn), lambda i, j: (i, 0, j)),
        out_shape=jax.ShapeDtypeStruct((depth, rows, n), F32),
        compiler_params=_cparams(("arbitrary", "arbitrary")),
        name="modulation",
    )(cc, mod_w, mod_b.reshape(depth, 1, n))


def _even_in_kernel(x_ref, xc_ref, m_ref, g1_ref, w_in_ref, cqg_ref, wuq_ref, qg_ref, ckvg_ref, wukv_ref, kg_ref,
                    cos_ref, sin_ref, q_ref, k_ref, v_ref, z_ref, *, n_lat_tiles):
    tm = x_ref.shape[1]
    m = m_ref[0]
    xc = jnp.concatenate([xc_ref[0]] * (tm // xc_ref.shape[1]), axis=0)
    x = jnp.where(pl.program_id(1) < n_lat_tiles, x_ref[0], xc)
    u = _mod_norm(x, g1_ref[...], m[0:1], m[1:2]).astype(BF16)
    hT = lax.dot_general(w_in_ref[...], u, NT_DIMS, preferred_element_type=F32)
    cos = cos_ref[...]
    sin = sin_ref[...]
    scale = A_QK ** -0.5 * LOG2_E
    half = A_ROPE // 2
    zpad = jnp.zeros((HEAD_PAD - A_QK, tm), F32)
    ones_rows = jnp.ones((BF16_ROWS, tm), F32)

    z_ref[0] = hT[A_IN:A_IN + B_WIDTH].T.astype(BF16)

    cqn = _rms_rows(hT[0:A_Q_LORA], cqg_ref[...]).astype(BF16)
    qT = jnp.dot(wuq_ref[...], cqn, preferred_element_type=F32)
    qg = qg_ref[...] * scale
    for h in range(A_HEADS):
        qn = _rms_rows(qT[h * A_QK:(h + 1) * A_QK], qg)
        o1, o2 = _rope_rows(qn[A_NOPE:A_NOPE + half], qn[A_NOPE + half:A_QK], cos, sin)
        full = jnp.concatenate([qn[0:A_NOPE], o1, o2, zpad], axis=0)
        q_ref[0, h] = full.astype(BF16)

    ckvn = _rms_rows(hT[A_Q_LORA:A_Q_LORA + A_KV_LORA], ckvg_ref[...]).astype(BF16)
    kvT = jnp.dot(wukv_ref[...], ckvn, preferred_element_type=F32)
    kr = hT[A_Q_LORA + A_KV_LORA:A_IN]
    kr_ss = jnp.sum(kr * kr, axis=0, keepdims=True)
    kg = kg_ref[...]
    hw = A_NOPE + A_V
    for h in range(A_HEADS):
        kn = kvT[h * hw:h * hw + A_NOPE]
        r = lax.rsqrt((jnp.sum(kn * kn, axis=0, keepdims=True) + kr_ss) * (1.0 / A_QK) + EPS)
        knn = kn * r * kg[0:A_NOPE]
        krn = kr * r * kg[A_NOPE:A_QK]
        o1, o2 = _rope_rows(krn[0:half], krn[half:A_ROPE], cos, sin)
        k_ref[0, h] = jnp.concatenate([knn, o1, o2, zpad], axis=0).T.astype(BF16)
        vh = kvT[h * hw + A_NOPE:(h + 1) * hw]
        v_ref[0, h] = jnp.concatenate([vh, ones_rows], axis=0).astype(BF16)


def _even_in(x, xc, xc_row0, n_ctx, m, g1, w_inT, cq_g, w_uqT, q_g, ckv_g, w_ukvT, k_g, cosT, sinT, n_lat):
    B, _, D = x.shape
    Lt = n_lat + n_ctx
    tm = IN_TILE
    assert n_lat % tm == 0 and tm % n_ctx == 0 and xc_row0 % n_ctx == 0
    nt = pl.cdiv(Lt, tm)
    n_lat_tiles = n_lat // tm
    full2 = lambda a: pl.BlockSpec(a.shape, lambda b, t: (0, 0))
    return pl.pallas_call(
        functools.partial(_even_in_kernel, n_lat_tiles=n_lat_tiles),
        grid=(B, nt),
        in_specs=[
            pl.BlockSpec((1, tm, D), lambda b, t: (b, jnp.minimum(t, n_lat_tiles - 1), 0)),
            pl.BlockSpec((1, n_ctx, D), lambda b, t: (b, xc_row0 // n_ctx, 0)),
            pl.BlockSpec((1, N_MOD, D), lambda b, t: (jnp.where(t < n_lat_tiles, b, B), 0, 0)),
            full2(g1), full2(w_inT), full2(cq_g), full2(w_uqT), full2(q_g), full2(ckv_g), full2(w_ukvT), full2(k_g),
            pl.BlockSpec((A_ROPE // 2, tm), lambda b, t: (0, t)),
            pl.BlockSpec((A_ROPE // 2, tm), lambda b, t: (0, t)),
        ],
        out_specs=[
            pl.BlockSpec((1, A_HEADS, HEAD_PAD, tm), lambda b, t: (b, 0, 0, t)),
            pl.BlockSpec((1, A_HEADS, tm, HEAD_PAD), lambda b, t: (b, 0, t, 0)),
            pl.BlockSpec((1, A_HEADS, A_V + BF16_ROWS, tm), lambda b, t: (b, 0, 0, t)),
            pl.BlockSpec((1, tm, B_WIDTH), lambda b, t: (b, t, 0)),
        ],
        out_shape=[
            jax.ShapeDtypeStruct((B, A_HEADS, HEAD_PAD, Lt), BF16),
            jax.ShapeDtypeStruct((B, A_HEADS, Lt, HEAD_PAD), BF16),
            jax.ShapeDtypeStruct((B, A_HEADS, A_V + BF16_ROWS, Lt), BF16),
            jax.ShapeDtypeStruct((B, Lt, B_WIDTH), BF16),
        ],
        compiler_params=_cparams(("parallel", "arbitrary")),
        name="even_in",
    )(x, xc, m, g1, w_inT, cq_g, w_uqT, q_g, ckv_g, w_ukvT, k_g, cosT, sinT)


def _mla_attn_kernel(*refs, aliased):
    q_ref, k_ref, v_ref, o_ref = refs[1:] if aliased else refs
    nk = k_ref.shape[2]
    nh = q_ref.shape[1]
    qs = [q_ref[0, j] for j in range(nh)]
    m = [None] * nh
    acc = [None] * nh
    pend = [[] for _ in range(nh)]
    chunks = [(k0, min(k0 + MLA_KEY_CHUNK, nk)) for k0 in range(0, nk, MLA_KEY_CHUNK)]

    def scores(j, k0, k1):
        s = jnp.dot(k_ref[0, j, k0:k1, :], qs[j], preferred_element_type=F32)
        mx = jnp.max(s, axis=0, keepdims=True)
        m_new = mx if m[j] is None else jnp.maximum(m[j], mx)
        alpha = None if m[j] is None else jnp.exp2(m[j] - m_new)
        pend[j].append((jnp.exp2(s - m_new).astype(BF16), alpha, k0, k1))
        m[j] = m_new

    def weighted_values(j):
        p, alpha, k0, k1 = pend[j].pop(0)
        pv = jnp.dot(v_ref[0, j, :, k0:k1], p, preferred_element_type=F32)
        acc[j] = pv if alpha is None else alpha * acc[j] + pv

    for i in range(len(chunks) + MLA_PV_LAG):
        for j in range(nh):
            if i < len(chunks):
                scores(j, *chunks[i])
            if i >= MLA_PV_LAG:
                weighted_values(j)
    outs = [(a / a[A_V:A_V + 1])[0:A_V] for a in acc]
    o_ref[0] = jnp.concatenate(outs, axis=0).T.astype(o_ref.dtype)


def _mla_attn(q, kT, vT, q_row0, n_q, tq, k_col0, n_k, nh, prev_out=None):
    B, H, _, Lt = q.shape
    assert q_row0 % tq == 0 and n_q % tq == 0 and k_col0 % n_k == 0 and H % nh == 0 and (nh * A_V) % LANES == 0
    qb, kb = q_row0 // tq, k_col0 // n_k
    aliased = prev_out is not None
    in_specs = [
        pl.BlockSpec((1, nh, HEAD_PAD, tq), lambda b, h, t: (b, h, 0, qb + t)),
        pl.BlockSpec((1, nh, n_k, HEAD_PAD), lambda b, h, t: (b, h, kb, 0)),
        pl.BlockSpec((1, nh, A_V + BF16_ROWS, n_k), lambda b, h, t: (b, h, 0, kb)),
    ]
    args = [q, kT, vT]
    if aliased:
        in_specs = [pl.BlockSpec(memory_space=pl.ANY)] + in_specs
        args = [prev_out] + args
    return pl.pallas_call(
        functools.partial(_mla_attn_kernel, aliased=aliased),
        grid=(B, H // nh, n_q // tq),
        in_specs=in_specs,
        out_specs=pl.BlockSpec((1, tq, nh * A_V), lambda b, h, t: (b, qb + t, h)),
        out_shape=jax.ShapeDtypeStruct((B, Lt, H * A_V), BF16),
        input_output_aliases={0: 0} if aliased else {},
        compiler_params=_cparams(("parallel", "arbitrary", "arbitrary")),
        name="mla_attn",
    )(*args)


def _fnet_kernel(*refs, norm, aliased):
    if aliased:
        refs = refs[1:]
    c_ref, s_ref, z_ref, cs_ref, wf_ref, o_ref, p_acc, q_acc = refs
    k = pl.program_id(2)

    @pl.when(k == 0)
    def _():
        p_acc[...] = jnp.zeros_like(p_acc)
        q_acc[...] = jnp.zeros_like(q_acc)

    c = c_ref[...]
    s = s_ref[...]
    for j in range(z_ref.shape[0]):
        z = z_ref[j]
        p_acc[j] += jnp.dot(c, z, preferred_element_type=F32)
        q_acc[j] += jnp.dot(s, z, preferred_element_type=F32)

    @pl.when(k == pl.num_programs(2) - 1)
    def _():
        for j in range(z_ref.shape[0]):
            pq = jnp.concatenate([p_acc[j], q_acc[j]], axis=1).astype(BF16)
            y = jnp.dot(pq, cs_ref[...], preferred_element_type=F32) * norm
            o_ref[j] = jnp.dot(y.astype(BF16), wf_ref[...], preferred_element_type=F32).astype(o_ref.dtype)


def _fnet(z, c_tab, s_tab, cs_bd, wf_bd, row0, out_rows, prev_out=None):
    B = z.shape[0]
    n = c_tab.shape[0]
    tm = min(n, 1024)
    tk = min(n, 1024)
    g = FNET_BGRP
    norm = 1.0 / math.sqrt(n * B_GDIM)
    aliased = prev_out is not None
    in_specs = [
        pl.BlockSpec((tm, tk), lambda b, i, k: (i, k)),
        pl.BlockSpec((tm, tk), lambda b, i, k: (i, k)),
        pl.BlockSpec((g, tk, B_WIDTH), lambda b, i, k: (b, row0 // tk + k, 0)),
        pl.BlockSpec(cs_bd.shape, lambda b, i, k: (0, 0)),
        pl.BlockSpec(wf_bd.shape, lambda b, i, k: (0, 0)),
    ]
    args = [c_tab, s_tab, z, cs_bd, wf_bd]
    if aliased:
        in_specs = [pl.BlockSpec(memory_space=pl.ANY)] + in_specs
        args = [prev_out] + args
    return pl.pallas_call(
        functools.partial(_fnet_kernel, norm=norm, aliased=aliased),
        grid=(B // g, n // tm, n // tk),
        in_specs=in_specs,
        out_specs=pl.BlockSpec((g, tm, B_WIDTH), lambda b, i, k: (b, row0 // tm + i, 0)),
        out_shape=jax.ShapeDtypeStruct((B, out_rows, B_WIDTH), BF16),
        scratch_shapes=[pltpu.VMEM((g, tm, B_WIDTH), F32), pltpu.VMEM((g, tm, B_WIDTH), F32)],
        input_output_aliases={0: 0} if aliased else {},
        compiler_params=_cparams(("parallel", "arbitrary", "arbitrary")),
        name="fnet",
    )(*args)


def _odd_in_kernel(x_ref, m_ref, g1_ref, w_in_ref, qg_ref, kg_ref, cos_ref, sin_ref, q_ref, k_ref, v_ref, pz_ref):
    tm = x_ref.shape[1]
    m = m_ref[0]
    u = _mod_norm(x_ref[0], g1_ref[...], m[0:1], m[1:2]).astype(BF16)
    hT = lax.dot_general(w_in_ref[...], u, NT_DIMS, preferred_element_type=F32)
    cos = cos_ref[...]
    sin = sin_ref[...]
    half = C_HDIM // 2
    zeros = jnp.zeros((C_HDIM, tm), F32)

    def norm_rope(xT, g_col):
        xn = _rms_rows(xT, g_col)
        o1, o2 = _rope_rows(xn[0:half], xn[half:C_HDIM], cos, sin)
        return jnp.concatenate([o1, o2], axis=0)

    qg = qg_ref[...] * (C_HDIM ** -0.5 * LOG2_E)
    for h in range(C_HEADS):
        qh = norm_rope(hT[h * C_HDIM:(h + 1) * C_HDIM], qg)
        parts = [qh, zeros] if (h // C_GROUP) % 2 == 0 else [zeros, qh]
        q_ref[0, h] = jnp.concatenate(parts, axis=0).astype(BF16)

    kg = kg_ref[...]
    for p in range(C_KV_HEADS // 2):
        ks = [norm_rope(hT[C_QW + (2 * p + e) * C_HDIM:C_QW + (2 * p + e + 1) * C_HDIM], kg) for e in range(2)]
        k_ref[0, p] = jnp.concatenate(ks, axis=0).T.astype(BF16)
        v0 = C_QW + C_KW + 2 * p * C_HDIM
        v_ref[0, p] = hT[v0:v0 + 2 * C_HDIM].T.astype(BF16)

    pz_ref[0] = hT[C_QW + 2 * C_KW:C_QW + 2 * C_KW + D_WIDTH].T


def _odd_in(x, m, g1, w_inT, q_g, k_g, cosT, sinT, n_lat_tiles):
    B, Lt, D = x.shape
    tm = IN_TILE
    nt = pl.cdiv(Lt, tm)
    n_lat_tiles = (n_lat_tiles * TOK_TILE) // tm
    full2 = lambda a: pl.BlockSpec(a.shape, lambda b, t: (0, 0))
    np_ = C_KV_HEADS // 2
    return pl.pallas_call(
        _odd_in_kernel,
        grid=(B, nt),
        in_specs=[
            pl.BlockSpec((1, tm, D), lambda b, t: (b, t, 0)),
            pl.BlockSpec((1, N_MOD, D), lambda b, t: (jnp.where(t < n_lat_tiles, b, B), 0, 0)),
            full2(g1), full2(w_inT), full2(q_g), full2(k_g),
            pl.BlockSpec((C_HDIM // 2, tm), lambda b, t: (0, t)),
            pl.BlockSpec((C_HDIM // 2, tm), lambda b, t: (0, t)),
        ],
        out_specs=[
            pl.BlockSpec((1, C_HEADS, HEAD_PAD, tm), lambda b, t: (b, 0, 0, t)),
            pl.BlockSpec((1, np_, tm, HEAD_PAD), lambda b, t: (b, 0, t, 0)),
            pl.BlockSpec((1, np_, tm, HEAD_PAD), lambda b, t: (b, 0, t, 0)),
            pl.BlockSpec((1, tm, D_WIDTH), lambda b, t: (b, t, 0)),
        ],
        out_shape=[
            jax.ShapeDtypeStruct((B, C_HEADS, HEAD_PAD, Lt), BF16),
            jax.ShapeDtypeStruct((B, np_, Lt, HEAD_PAD), BF16),
            jax.ShapeDtypeStruct((B, np_, Lt, HEAD_PAD), BF16),
            jax.ShapeDtypeStruct((B, Lt, D_WIDTH), F32),
        ],
        compiler_params=_cparams(("parallel", "arbitrary")),
        name="odd_in",
    )(x, m, g1, w_inT, q_g, k_g, cosT, sinT)


def _win_attn_kernel(sink_ref, q_ref, k_ref, v_ref, o_ref, *, n_lat_tiles, n_lat):
    t = pl.program_id(1)
    npair = k_ref.shape[1]
    nh = q_ref.shape[1] // npair
    tq = q_ref.shape[3]
    half = nh // 2
    ck = WIN_KEY_CHUNK
    col = lax.broadcasted_iota(jnp.int32, (1, 2 * tq), 1)
    row = lax.broadcasted_iota(jnp.int32, (HEAD_PAD, tq), 0)

    def run(pair, chunks):
        qg = [jnp.concatenate([q_ref[0, pair * nh + j], q_ref[0, pair * nh + half + j]], axis=1)
              for j in range(half)]
        sk = [jnp.where(col < tq, sink_ref[pair * nh + j], sink_ref[pair * nh + half + j]) * LOG2_E
              for j in range(half)]
        m = list(sk)
        acc = [None] * half
        pend = [[] for _ in range(half)]

        def scores(j, kb, valid):
            s = jnp.dot(kb, qg[j], preferred_element_type=F32)
            if valid is not None:
                s = jnp.where(valid, s, NEG_INF)
            m_new = jnp.maximum(m[j], jnp.max(s, axis=0, keepdims=True))
            pend[j].append((jnp.exp2(s - m_new).astype(BF16), jnp.exp2(m[j] - m_new)))
            m[j] = m_new

        def weighted_values(j, vbT):
            p, alpha = pend[j].pop(0)
            pv = jnp.dot(vbT, p, preferred_element_type=F32)
            acc[j] = pv if acc[j] is None else alpha * acc[j] + pv

        for i in range(len(chunks) + 1):
            for j in range(half):
                if i < len(chunks):
                    scores(j, chunks[i][0], chunks[i][2])
                if i >= 1:
                    weighted_values(j, chunks[i - 1][1])
        for j in range(half):
            l = acc[j][HEAD_PAD:HEAD_PAD + 1] + jnp.exp2(sk[j] - m[j])
            o = acc[j][0:HEAD_PAD] / l
            blk = jnp.where(row < C_HDIM, o[:, 0:tq], o[:, tq:2 * tq])
            c0 = (pair * half + j) * HEAD_PAD
            o_ref[0, :, c0:c0 + HEAD_PAD] = blk.T.astype(o_ref.dtype)

    def kv_chunk(pair, start):
        kb = k_ref[0, pair, pl.ds(start, ck), :]
        vT = v_ref[0, pair, pl.ds(start, ck), :].astype(F32).T
        return kb, jnp.concatenate([vT, jnp.ones((BF16_ROWS, ck), F32)], axis=0).astype(BF16)

    @pl.when(t < n_lat_tiles)
    def _():
        start = pl.multiple_of(jnp.maximum(t * tq - C_WINDOW, 0), C_WINDOW)
        qpos = t * tq + (lax.broadcasted_iota(jnp.int32, (ck, 2 * tq), 1) & (tq - 1))
        valids = []
        for c in range((tq + 2 * C_WINDOW) // ck):
            kpos = start + c * ck + lax.broadcasted_iota(jnp.int32, (ck, 2 * tq), 0)
            valids.append((jnp.abs(kpos - qpos) <= C_WINDOW) & (kpos < n_lat))
        for pair in range(npair):
            chunks = [kv_chunk(pair, start + c * ck) + (valid,) for c, valid in enumerate(valids)]
            run(pair, chunks + [kv_chunk(pair, n_lat) + (None,)])

    @pl.when(t >= n_lat_tiles)
    def _():
        for pair in range(npair):
            run(pair, [kv_chunk(pair, n_lat) + (None,)])


def _win_attn(sink, q, k2, v2, n_lat, ctx_queries):
    B, H, _, Lt = q.shape
    tq = TOK_TILE
    npair = k2.shape[1]
    return pl.pallas_call(
        functools.partial(_win_attn_kernel, n_lat_tiles=n_lat // tq, n_lat=n_lat),
        grid=(B, (Lt if ctx_queries else n_lat) // tq),
        in_specs=[
            pl.BlockSpec(memory_space=pltpu.SMEM),
            pl.BlockSpec((1, H, HEAD_PAD, tq), lambda b, t: (b, 0, 0, t)),
            pl.BlockSpec((1, npair, Lt, HEAD_PAD), lambda b, t: (b, 0, 0, 0)),
            pl.BlockSpec((1, npair, Lt, HEAD_PAD), lambda b, t: (b, 0, 0, 0)),
        ],
        out_specs=pl.BlockSpec((1, tq, H * C_HDIM), lambda b, t: (b, t, 0)),
        out_shape=jax.ShapeDtypeStruct((B, Lt, H * C_HDIM), BF16),
        compiler_params=_cparams(("parallel", "arbitrary")),
        name="win_attn",
    )(sink, q, k2, v2)


def _pool_lane_tile(z, wp, ps, g0):
    n = z.shape[0]
    row = lax.broadcasted_iota(jnp.int32, z.shape, 0)
    low = lax.broadcasted_iota(jnp.int32, z.shape, 1) < D_GDIM

    def shift_down(a, s):
        return jnp.where(row >= s, pltpu.roll(a, s, 0), 0.0)

    def shift_up(a, s):
        return jnp.where(row < n - s, pltpu.roll(a, n - s, 0), 0.0)

    h_lo, h_hi = D_WINDOWS[g0] // 2, D_WINDOWS[g0 + 1] // 2
    back, fwd, h = z, z, 1
    levels = {1: (z, z)}
    while h < h_hi:
        back = back + shift_down(back, h)
        fwd = fwd + shift_up(fwd, h)
        h *= 2
        levels[h] = (back, fwd)
    back_sel = jnp.where(low, levels[h_lo][0], levels[h_hi][0])
    fwd_sel = jnp.where(low, levels[h_lo][1], levels[h_hi][1])
    win_sum = shift_down(back_sel, 1) + fwd_sel
    half = jnp.where(low, h_lo, h_hi)
    cnt = jnp.minimum(row + half, n) - jnp.maximum(row - half, 0)
    pooled = (win_sum / cnt.astype(F32) - z).astype(BF16)
    return jnp.dot(pooled, wp, preferred_element_type=F32) * ps


def _pool_kernel(pz_ref, wp_ref, ps_ref, o_ref, *, n_lat):
    lt = pz_ref.shape[1]
    for lo, hi in ((0, n_lat), (n_lat, lt)):
        for c in range(D_GROUPS // 2):
            c0, c1 = c * LANES, (c + 1) * LANES
            y = _pool_lane_tile(pz_ref[0, lo:hi, c0:c1], wp_ref[c0:c1, c0:c1], ps_ref[:, c0:c1], 2 * c)
            o_ref[0, lo:hi, c0:c1] = y.astype(o_ref.dtype)


def _pool(pz, wp_bd, p_scale, n_lat):
    B, Lt, W = pz.shape
    return pl.pallas_call(
        functools.partial(_pool_kernel, n_lat=n_lat),
        grid=(B,),
        in_specs=[
            pl.BlockSpec((1, Lt, W), lambda b: (b, 0, 0)),
            pl.BlockSpec(wp_bd.shape, lambda b: (0, 0)),
            pl.BlockSpec(p_scale.shape, lambda b: (0, 0)),
        ],
        out_specs=pl.BlockSpec((1, Lt, W), lambda b: (b, 0, 0)),
        out_shape=jax.ShapeDtypeStruct((B, Lt, W), BF16),
        compiler_params=_cparams(("parallel",)),
        name="pool",
    )(pz, wp_bd, p_scale)


def _mix_ffn_kernel(*refs, d_ff, aliased):
    if aliased:
        refs = refs[1:]
    (x_ref, xp_ref, xn_ref, a_ref, ap_ref, an_ref, b_ref, bp_ref, bn_ref, m_ref, g2_ref, wa_ref, wb_ref,
     wup_ref, cw_ref, cb_ref, wdn_ref, o_ref, a_scr, b_scr, x1_scr, u_scr, act_scr) = refs
    t = pl.program_id(1)
    tm = x_ref.shape[1]
    hx = FFN_HALO
    hm = BF16_ROWS
    rows = tm + 2 * hx
    m = m_ref[0]
    g2 = g2_ref[...]

    for src, prv, nxt, scr in ((a_ref, ap_ref, an_ref, a_scr), (b_ref, bp_ref, bn_ref, b_scr)):
        scr[0:tm] = src[0]
        scr[tm:tm + hm] = nxt[0]
        scr[tm + hm:tm + 2 * hm] = prv[0]
    gate1 = m[2:3]

    def mod_norm_rows(xv):
        return _mod_norm(xv, g2, m[3:4], m[4:5])

    def out_proj(r0, r1):
        return (jnp.dot(a_scr[r0:r1], wa_ref[...], preferred_element_type=F32)
                + jnp.dot(b_scr[r0:r1], wb_ref[...], preferred_element_type=F32))

    nblk = max(tm // FFN_ROW_BLOCK, 1)
    rb = tm // nblk
    for i in range(nblk):
        r0, r1 = i * rb, (i + 1) * rb
        x1 = x_ref[0, r0:r1, :] + gate1 * out_proj(r0, r1)
        x1_scr[r0:r1] = x1
        u_scr[r0:r1] = mod_norm_rows(x1).astype(BF16)
    yh = out_proj(tm, tm + 2 * hm)
    x1_next = xn_ref[0] + gate1 * yh[0:hx]
    x1_prev = xp_ref[0] + gate1 * yh[2 * hm - hx:2 * hm]
    u_next = jnp.where(t == pl.num_programs(1) - 1, 0.0, mod_norm_rows(x1_next))
    u_prev = jnp.where(t == 0, 0.0, mod_norm_rows(x1_prev))
    u_scr[tm:rows] = jnp.concatenate([u_next, u_prev], axis=0).astype(BF16)

    def conv(h, col0, width):
        cw = cw_ref[:, col0:col0 + width]
        out = (pltpu.roll(h, 1, 0) * cw[0:1] + h * cw[1:2] + pltpu.roll(h, rows - 1, 0) * cw[2:3]
               + cb_ref[:, col0:col0 + width])
        return out[0:tm]

    u = u_scr[...]
    for c in range(d_ff // FFN_CHUNK):
        c0 = c * FFN_CHUNK
        gate = conv(jnp.dot(u, wup_ref[:, c0:c0 + FFN_CHUNK], preferred_element_type=F32), c0, FFN_CHUNK)
        val = conv(jnp.dot(u, wup_ref[:, d_ff + c0:d_ff + c0 + FFN_CHUNK], preferred_element_type=F32),
                   d_ff + c0, FFN_CHUNK)
        act_scr[:, c0:c0 + FFN_CHUNK] = (gate * jax.nn.sigmoid(gate) * val).astype(BF16)

    for i in range(nblk):
        r0, r1 = i * rb, (i + 1) * rb
        y2 = jnp.dot(act_scr[r0:r1], wdn_ref[...], preferred_element_type=F32)
        o_ref[0, r0:r1, :] = x1_scr[r0:r1] + m[5:6] * y2


def _mix_ffn(x, x_row0, att, br, m, wa, wb, g2, w_up, conv_w, conv_b, w_down, row0, n_rows, tm, ctx_mod, out_rows,
             prev_out=None):
    B, x_rows, D = x.shape
    Lt = att.shape[1]
    d_ff = w_down.shape[0]
    assert row0 % tm == 0 and x_row0 % tm == 0 and n_rows % tm == 0 and tm % BF16_ROWS == 0
    assert 2 * FFN_HALO == BF16_ROWS

    def tile(w, r0=row0):
        return pl.BlockSpec((1, tm, w), lambda b, t: (b, r0 // tm + t, 0))

    def halo(w, h, prev, r0=row0, total=Lt):
        per, first, n = tm // h, r0 // h, total // h
        if prev:
            return pl.BlockSpec((1, h, w), lambda b, t: (b, jnp.maximum(first + t * per - 1, 0), 0))
        return pl.BlockSpec((1, h, w), lambda b, t: (b, jnp.minimum(first + (t + 1) * per, n - 1), 0))

    const = lambda a: pl.BlockSpec(a.shape, lambda b, t: (0, 0))
    once = lambda a: pl.BlockSpec(a.shape, lambda b, t: (0, 0), pipeline_mode=pl.Buffered(1))
    wa_w, wb_w = att.shape[2], br.shape[2]
    aliased = prev_out is not None
    in_specs = [
        tile(D, x_row0), halo(D, FFN_HALO, True, x_row0, x_rows), halo(D, FFN_HALO, False, x_row0, x_rows),
        tile(wa_w), halo(wa_w, BF16_ROWS, True), halo(wa_w, BF16_ROWS, False),
        tile(wb_w), halo(wb_w, BF16_ROWS, True), halo(wb_w, BF16_ROWS, False),
        pl.BlockSpec((1, N_MOD, D), (lambda b, t: (B, 0, 0)) if ctx_mod else (lambda b, t: (b, 0, 0))),
        const(g2), once(wa), once(wb), once(w_up), const(conv_w), const(conv_b), once(w_down),
    ]
    args = [x, x, x, att, att, att, br, br, br, m, g2, wa, wb, w_up, conv_w, conv_b, w_down]
    if aliased:
        in_specs = [pl.BlockSpec(memory_space=pl.ANY)] + in_specs
        args = [prev_out] + args
    return pl.pallas_call(
        functools.partial(_mix_ffn_kernel, d_ff=d_ff, aliased=aliased),
        grid=(B, n_rows // tm),
        in_specs=in_specs,
        out_specs=tile(D),
        out_shape=jax.ShapeDtypeStruct((B, out_rows, D), F32),
        scratch_shapes=[
            pltpu.VMEM((tm + 2 * BF16_ROWS, wa_w), BF16), pltpu.VMEM((tm + 2 * BF16_ROWS, wb_w), BF16),
            pltpu.VMEM((tm, D), F32), pltpu.VMEM((tm + 2 * FFN_HALO, D), BF16), pltpu.VMEM((tm, d_ff), BF16)],
        input_output_aliases={0: 0} if aliased else {},
        compiler_params=_cparams(("parallel", "arbitrary")),
        name="mix_ffn",
    )(*args)


def _rope_tables_T(rows, rot_dim, n_ctx):
    n_freq = rot_dim // 4
    inv = ROPE_BASE ** (-jnp.arange(n_freq, dtype=F32) / n_freq)
    row = jnp.broadcast_to(jnp.arange(rows, dtype=F32)[:, None], (rows, GRID_W)).reshape(-1)
    col = jnp.broadcast_to(jnp.arange(GRID_W, dtype=F32)[None, :], (rows, GRID_W)).reshape(-1)
    ang = jnp.concatenate([row[:, None] * inv, col[:, None] * inv], axis=-1)
    cosT = jnp.concatenate([jnp.cos(ang).T, jnp.ones((rot_dim // 2, n_ctx), F32)], axis=1)
    sinT = jnp.concatenate([jnp.sin(ang).T, jnp.zeros((rot_dim // 2, n_ctx), F32)], axis=1)
    return cosT, sinT


def _dft_tables(n):
    idx = (jnp.arange(n, dtype=jnp.int32)[:, None] * jnp.arange(n, dtype=jnp.int32)[None, :]) % n
    ang = idx.astype(F32) * (2.0 * math.pi / n)
    return jnp.cos(ang), jnp.sin(ang)


def _dft_tables_split(n, r):
    l = jnp.arange(n, dtype=jnp.int32)[:, None]
    kk = jnp.arange(r, dtype=jnp.int32)[None, :]
    a = ((kk * l) % r).astype(F32) * (2.0 * math.pi / r)
    b = ((kk * l) % n).astype(F32) * (2.0 * math.pi / n)
    ca, sa, cb, sb = jnp.cos(a)[:, :, None], jnp.sin(a)[:, :, None], jnp.cos(b)[:, None, :], jnp.sin(b)[:, None, :]
    return (ca * cb - sa * sb).reshape(n, n), (sa * cb + ca * sb).reshape(n, n)


def _block_diag(w):
    g, a, b = w.shape
    eye = jnp.eye(g, dtype=w.dtype)
    return (eye[:, None, :, None] * w[:, :, None, :]).reshape(g * a, g * b)


def _col(v):
    return v.reshape(-1, 1).astype(F32)


def kernel(x, c, ctx, c_ctx, mod_w, mod_b, norm1_g, norm2_g, mla_w_in, mla_cq_g, mla_ckv_g, mla_w_uq, mla_w_ukv,
           mla_q_g, mla_k_g, fnet_w, even_w_out, win_w_in, win_q_g, win_k_g, win_sink, pool_w, pool_scale, odd_w_out,
           ffn_up, ffn_conv_w, ffn_conv_b, ffn_down):
    B, L, D = x.shape
    C = ctx.shape[1]
    depth = mod_w.shape[0]
    assert L % TOK_TILE == 0 and C == TOK_TILE and B % FNET_BGRP == 0
    n_lat_tiles = L // TOK_TILE
    rows = L // GRID_W

    pad = (-(B + 1)) % 8
    cc = jnp.concatenate([c, c_ctx[None, :], jnp.zeros((pad, D), F32)], axis=0)
    mods = _modulation(cc, mod_w, mod_b).reshape(depth, B + 1 + pad, N_MOD, D)

    cos_a, sin_a = _rope_tables_T(rows, A_ROPE, C)
    cos_w, sin_w = _rope_tables_T(rows, C_HDIM, C)
    r_lat = math.isqrt(L)
    c_lat, s_lat = (t.astype(BF16) for t in (_dft_tables_split(L, r_lat) if r_lat * r_lat == L else _dft_tables(L)))
    c_ctx_t, s_ctx_t = (t.astype(BF16) for t in _dft_tables(C))
    cc64, ss64 = _dft_tables(B_GDIM)
    eye_g = jnp.eye(B_GROUPS, dtype=F32)
    cs_bd = jnp.concatenate([jnp.kron(eye_g, cc64), -jnp.kron(eye_g, ss64)], axis=0).astype(BF16)

    nh = C_HEADS // (C_KV_HEADS // 2)
    head_order = [p * nh + e * (nh // 2) + j for p in range(C_KV_HEADS // 2) for j in range(nh // 2) for e in range(2)]
    att_perm = jnp.asarray([h * C_HDIM + d for h in head_order for d in range(C_HDIM)], jnp.int32)

    h = None
    for i in range(depth):
        j = i // 2
        m = mods[i]
        g1 = norm1_g[i][None, :]
        src = (x, ctx, 0) if h is None else (h, h, L)
        if i % 2 == 0:
            q, kT, vT, z = _even_in(
                *src, C, m, g1, mla_w_in[j].T.astype(BF16), _col(mla_cq_g[j]), mla_w_uq[j].T.astype(BF16),
                _col(mla_q_g[j]), _col(mla_ckv_g[j]), mla_w_ukv[j].T.astype(BF16), _col(mla_k_g[j]),
                cos_a, sin_a, L)
            att = _mla_attn(q, kT, vT, 0, L, MLA_Q_TILE, 0, L + C, MLA_HEADS)
            att = _mla_attn(q, kT, vT, L, C, C, L, C, MLA_HEADS, prev_out=att)
            wf_bd = _block_diag(fnet_w[j]).astype(BF16)
            br = _fnet(z, c_lat, s_lat, cs_bd, wf_bd, 0, L + C)
            br = _fnet(z, c_ctx_t, s_ctx_t, cs_bd, wf_bd, L, L + C, prev_out=br)
            w_out = even_w_out[j].astype(BF16)
            wa, wb = w_out[:A_HEADS * A_V], w_out[A_HEADS * A_V:]
        else:
            q, k2, v2, pz = _odd_in(h, m, g1, win_w_in[j].T.astype(BF16), _col(win_q_g[j]), _col(win_k_g[j]),
                                    cos_w, sin_w, n_lat_tiles)
            att = _win_attn(win_sink[j].reshape(-1), q, k2, v2, L, i < depth - 1)
            br = _pool(pz, _block_diag(pool_w[j]).astype(BF16), pool_scale[j][None, :], L)
            w_out = odd_w_out[j].astype(BF16)
            wa, wb = w_out[:C_QW][att_perm], w_out[C_QW:]
        ffn_args = (att, br, m, wa, wb, norm2_g[i][None, :], ffn_up[i].astype(BF16), ffn_conv_w[i],
                    ffn_conv_b[i][None, :], ffn_down[i].astype(BF16))
        if i < depth - 1:
            h_new = _mix_ffn(src[0], 0, *ffn_args, 0, L, FFN_LAT_TILE, False, L + C)
            h = _mix_ffn(src[1], src[2], *ffn_args, L, C, C, True, L + C, prev_out=h_new)
        else:
            h = _mix_ffn(src[0], 0, *ffn_args, 0, L, FFN_LAT_TILE, False, L)
    return h
```
